```python
import jax
import jax.numpy as jnp
from jax import lax
import numpy as np

D_MODEL = 1024
BATCH = 8
SEQ = 4096
DEPTH = 2

GRID_W = 64
CTX_LEN = 256
HEAD_DIM = 64
MIX_HALF = D_MODEL // 2
N_MOD = 9
D_FF = ((8 * D_MODEL // 3 + 127) // 128) * 128
EPS = 1e-6
ROPE_THETA = 10000.0
NEG = -1e30
N_EVEN = (DEPTH + 1) // 2
N_ODD = DEPTH // 2
A_HEADS = MIX_HALF // HEAD_DIM
A_KV_HEADS = A_HEADS // 4
WINDOW = 128
A_BLOCK = 128
A_W = A_HEADS * HEAD_DIM
A_KV = A_KV_HEADS * HEAD_DIM
B_GROUPS = MIX_HALF // HEAD_DIM
B_CHUNK = 128
B_W = B_GROUPS * HEAD_DIM
EVEN_SPLIT = [A_W, A_W + A_KV, A_W + 2 * A_KV, A_W + 2 * A_KV + B_W]
EVEN_IN = A_W + 2 * A_KV + 2 * B_W
EVEN_MIX = A_W + B_W
POOL_WINDOWS = (2, 4, 8, 16)
C_GROUPS = len(POOL_WINDOWS)
C_W = MIX_HALF
C_GROUP_W = C_W // C_GROUPS
D_HEADS = MIX_HALF // HEAD_DIM
D_W = D_HEADS * HEAD_DIM
NA_ROWS = 8
NA_COLS = 16
ODD_SPLIT = [C_W, C_W + D_W, C_W + 2 * D_W]
ODD_IN = C_W + 3 * D_W
ODD_MIX = C_W + D_W

kernel_name = 'hybrid_dit_prefix_trunk'


def rms_norm(x, g):
    xf = x.astype(jnp.float32)
    y = xf * lax.rsqrt(jnp.mean(xf * xf, axis=-1, keepdims=True) + EPS)
    return (y * g.astype(jnp.float32)).astype(x.dtype)


def norm_mod(h, g, shift, scale):
    return rms_norm(h, g) * (1.0 + scale) + shift


def swiglu(x, w_gu, w_down):
    gt, up = jnp.split(x @ w_gu, 2, axis=-1)
    return (jax.nn.silu(gt) * up) @ w_down


def half_ffn(h, g, shift, scale, gate, w_gu, w_down):
    return h + 0.5 * gate * swiglu(norm_mod(h, g, shift, scale), w_gu, w_down)


def axial_rope_angles(s):
    t = jnp.arange(s)
    row = (t // GRID_W).astype(jnp.float32)
    col = (t % GRID_W).astype(jnp.float32)
    m = HEAD_DIM // 4
    inv = 1.0 / (ROPE_THETA ** (jnp.arange(m, dtype=jnp.float32) / m))
    return row[:, None] * inv[None, :], col[:, None] * inv[None, :]


def rotate(x, ang):
    x1, x2 = jnp.split(x, 2, axis=-1)
    cos = jnp.cos(ang)[None, :, None, :]
    sin = jnp.sin(ang)[None, :, None, :]
    return jnp.concatenate([x1 * cos - x2 * sin, x2 * cos + x1 * sin], axis=-1)


def apply_rope_2d(x, ang_r, ang_c):
    half = HEAD_DIM // 2
    return jnp.concatenate([rotate(x[..., :half], ang_r), rotate(x[..., half:], ang_c)], axis=-1).astype(x.dtype)


def ctx_attention(q, k, v, sink):
    b, l, nq, dh = q.shape
    nkv = k.shape[2]
    g = nq // nkv
    qg = q.reshape(b, l, nkv, g, dh)
    sc = jnp.einsum('bqkgd,bskd->bkgqs', qg, k).astype(jnp.float32) * dh ** -0.5
    if sink is not None:
        s_sink = jnp.broadcast_to(sink.astype(jnp.float32).reshape(nkv, g)[None, :, :, None, None], sc.shape[:-1] + (1,))
        sc = jnp.concatenate([sc, s_sink], axis=-1)
    p = jax.nn.softmax(sc, axis=-1)[..., :l].astype(v.dtype)
    o = jnp.einsum('bkgqs,bskd->bqkgd', p, v)
    return o.reshape(b, l, nq * dh)


def window_gqa(q, k, v, k_ctx, v_ctx, sink):
    b, s, nq, dh = q.shape
    nkv = k.shape[2]
    g = nq // nkv
    nb = s // A_BLOCK
    lc = k_ctx.shape[1]
    nloc = 3 * A_BLOCK
    qb = q.reshape(b, nb, A_BLOCK, nkv, g, dh)

    def band(t):
        tb = t.reshape(b, nb, A_BLOCK, nkv, dh)
        pad = jnp.zeros_like(tb[:, :1])
        prev = jnp.concatenate([pad, tb[:, :-1]], axis=1)
        nxt = jnp.concatenate([tb[:, 1:], pad], axis=1)
        return jnp.concatenate([prev, tb, nxt], axis=2)

    kb, vb = band(k), band(v)
    scale = dh ** -0.5
    qpos = jnp.arange(nb)[:, None, None] * A_BLOCK + jnp.arange(A_BLOCK)[None, :, None]
    kpos = (jnp.arange(nb)[:, None, None] - 1) * A_BLOCK + jnp.arange(nloc)[None, None, :]
    valid = (jnp.abs(qpos - kpos) <= WINDOW) & (kpos >= 0) & (kpos < s)
    s_loc = jnp.einsum('bnqkgd,bnskd->bnkgqs', qb, kb).astype(jnp.float32) * scale
    s_loc = jnp.where(valid[None, :, None, None], s_loc, NEG)
    s_ctx = jnp.einsum('bnqkgd,bckd->bnkgqc', qb, k_ctx).astype(jnp.float32) * scale
    s_sink = jnp.broadcast_to(sink.astype(jnp.float32).reshape(nkv, g)[None, None, :, :, None, None], s_loc.shape[:-1] + (1,))
    p = jax.nn.softmax(jnp.concatenate([s_loc, s_ctx, s_sink], axis=-1), axis=-1).astype(v.dtype)
    o = (jnp.einsum('bnkgqs,bnskd->bnqkgd', p[..., :nloc], vb)
         + jnp.einsum('bnkgqc,bckd->bnqkgd', p[..., nloc:nloc + lc], v_ctx))
    return o.reshape(b, s, nq * dh)


def chunk_gmlp(bu, bv, v_gain, ws, bias):
    b, s, _ = bu.shape
    u = jax.nn.gelu(bu)
    vv = jax.nn.gelu(bv).reshape(b, s // B_CHUNK, B_CHUNK, B_GROUPS, HEAD_DIM)
    vv = rms_norm(vv, v_gain.reshape(B_GROUPS, HEAD_DIM))
    mixed = jnp.einsum('gij,bnjgd->bnigd', ws, vv) + bias.T[None, None, :, :, None]
    return u * mixed.reshape(b, s, B_W)


def multiscale_pool(xp, w_pool, c_scale):
    b, s, _ = xp.shape
    xf = xp.astype(jnp.float32)
    cs = jnp.concatenate([jnp.zeros_like(xf[:, :1]), jnp.cumsum(xf, axis=1)], axis=1)
    t = jnp.arange(s)
    outs = []
    for gi, w in enumerate(POOL_WINDOWS):
        lo = jnp.clip(t - w // 2, 0, s)
        hi = jnp.clip(t + w - w // 2, 0, s)
        csg = cs[..., gi * C_GROUP_W:(gi + 1) * C_GROUP_W]
        mean = (csg[:, hi] - csg[:, lo]) / (hi - lo).astype(jnp.float32)[None, :, None]
        outs.append(mean - xf[..., gi * C_GROUP_W:(gi + 1) * C_GROUP_W])
    pooled = jnp.stack(outs, axis=2).astype(xp.dtype)
    y = jnp.einsum('bsgc,gcd->bsgd', pooled, w_pool).reshape(b, s, C_W)
    return y * c_scale


def neighbourhood_attn(q, k, v, k_ctx, v_ctx, rpb):
    b, s, nh, dh = q.shape
    rows = s // GRID_W
    kr = min(NA_ROWS, rows)
    nqb = GRID_W // NA_COLS
    kbw = 2 * NA_COLS
    q5 = q.reshape(b, rows, GRID_W, nh, dh)
    k5 = k.reshape(b, rows, GRID_W, nh, dh)
    v5 = v.reshape(b, rows, GRID_W, nh, dh)
    qcol = np.arange(GRID_W).reshape(nqb, NA_COLS)
    qstart = np.clip(qcol - NA_COLS // 2, 0, GRID_W - NA_COLS)
    kstart = np.clip(np.arange(nqb) * NA_COLS - NA_COLS // 2, 0, GRID_W - kbw)
    kcol = kstart[:, None] + np.arange(kbw)[None, :]
    col_valid = (kcol[:, None, :] >= qstart[:, :, None]) & (kcol[:, None, :] < qstart[:, :, None] + NA_COLS)
    col_idx = np.clip(kcol[:, None, :] - qcol[:, :, None], 1 - NA_COLS, NA_COLS - 1) + NA_COLS - 1
    scale = dh ** -0.5
    nloc = kr * kbw

    def row_block(r):
        r0 = jnp.clip(r - kr // 2, 0, rows - kr)
        qr = lax.dynamic_index_in_dim(q5, r, axis=1, keepdims=False).reshape(b, nqb, NA_COLS, nh, dh)
        kblk = lax.dynamic_slice_in_dim(k5, r0, kr, axis=1)[:, :, kcol]
        vblk = lax.dynamic_slice_in_dim(v5, r0, kr, axis=1)[:, :, kcol]
        row_idx = r0 + jnp.arange(kr) - r + NA_ROWS - 1
        bias = jnp.transpose(rpb[:, row_idx][:, :, col_idx], (0, 2, 3, 1, 4))
        s_loc = jnp.einsum('bmqhd,brmkhd->bhmqrk', qr, kblk).astype(jnp.float32) * scale + bias[None].astype(jnp.float32)
        s_loc = jnp.where(col_valid[None, None, :, :, None, :], s_loc, NEG)
        s_ctx = jnp.einsum('bmqhd,bchd->bhmqc', qr, k_ctx).astype(jnp.float32) * scale
        p = jax.nn.softmax(jnp.concatenate([s_loc.reshape(b, nh, nqb, NA_COLS, nloc), s_ctx], axis=-1), axis=-1).astype(v.dtype)
        o = (jnp.einsum('bhmqrk,brmkhd->bmqhd', p[..., :nloc].reshape(s_loc.shape), vblk)
             + jnp.einsum('bhmqc,bchd->bmqhd', p[..., nloc:], v_ctx))
        return o.reshape(b, GRID_W, nh, dh)

    out = lax.map(row_block, jnp.arange(rows))
    return jnp.moveaxis(out, 0, 1).reshape(b, s, nh * dh)


def even_mixer(z, zc, w_in, w_out, q_gain, k_gain, sink, v_gain, ws, bias, ang_r, ang_c, need_ctx):
    b, s, _ = z.shape
    l = zc.shape[1]
    q, k, v, bu, bv = jnp.split(z @ w_in, EVEN_SPLIT, axis=-1)
    q = apply_rope_2d(rms_norm(q.reshape(b, s, A_HEADS, HEAD_DIM), q_gain), ang_r, ang_c)
    k = apply_rope_2d(rms_norm(k.reshape(b, s, A_KV_HEADS, HEAD_DIM), k_gain), ang_r, ang_c)
    v = v.reshape(b, s, A_KV_HEADS, HEAD_DIM)
    if need_ctx:
        qc, kc, vc, buc, bvc = jnp.split(zc @ w_in, EVEN_SPLIT, axis=-1)
    else:
        kc, vc = jnp.split(zc @ w_in[:, A_W:A_W + 2 * A_KV], 2, axis=-1)
    kc = rms_norm(kc.reshape(b, l, A_KV_HEADS, HEAD_DIM), k_gain)
    vc = vc.reshape(b, l, A_KV_HEADS, HEAD_DIM)
    y = jnp.concatenate([window_gqa(q, k, v, kc, vc, sink), chunk_gmlp(bu, bv, v_gain, ws, bias)], axis=-1) @ w_out
    yc = None
    if need_ctx:
        qc = rms_norm(qc.reshape(b, l, A_HEADS, HEAD_DIM), q_gain)
        yc = jnp.concatenate([ctx_attention(qc, kc, vc, sink), chunk_gmlp(buc, bvc, v_gain, ws, bias)], axis=-1) @ w_out
    return y, yc


def odd_mixer(z, zc, w_in, w_out, w_pool, c_scale, q_gain, k_gain, rpb, need_ctx):
    b, s, _ = z.shape
    l = zc.shape[1]
    xp, q, k, v = jnp.split(z @ w_in, ODD_SPLIT, axis=-1)
    q = rms_norm(q.reshape(b, s, D_HEADS, HEAD_DIM), q_gain)
    k = rms_norm(k.reshape(b, s, D_HEADS, HEAD_DIM), k_gain)
    v = v.reshape(b, s, D_HEADS, HEAD_DIM)
    if need_ctx:
        xpc, qc, kc, vc = jnp.split(zc @ w_in, ODD_SPLIT, axis=-1)
    else:
        kc, vc = jnp.split(zc @ w_in[:, C_W + D_W:], 2, axis=-1)
    kc = rms_norm(kc.reshape(b, l, D_HEADS, HEAD_DIM), k_gain)
    vc = vc.reshape(b, l, D_HEADS, HEAD_DIM)
    y = jnp.concatenate([multiscale_pool(xp, w_pool, c_scale), neighbourhood_attn(q, k, v, kc, vc, rpb)], axis=-1) @ w_out
    yc = None
    if need_ctx:
        qc = rms_norm(qc.reshape(b, l, D_HEADS, HEAD_DIM), q_gain)
        yc = jnp.concatenate([multiscale_pool(xpc, w_pool, c_scale), ctx_attention(qc, kc, vc, None)], axis=-1) @ w_out
    return y, yc


def setup_inputs(seed: int = 0) -> dict:
    key = jax.random.key(seed)
    ks = jax.random.split(key, 24)
    f32 = jnp.float32

    def nrm(k, shape, sd):
        return jax.random.normal(k, shape, f32) * sd

    d = D_MODEL
    return {
        'x': nrm(ks[0], (BATCH, SEQ, d), 1.0),
        'c': nrm(ks[1], (BATCH, d), 1.0),
        'ctx': nrm(ks[2], (BATCH, CTX_LEN, d), 1.0),
        'c_ctx': nrm(ks[3], (d,), 1.0),
        'ada_w': nrm(ks[4], (DEPTH, d, N_MOD * d), 0.5 * d ** -0.5),
        'ada_b': nrm(ks[5], (DEPTH, N_MOD * d), 0.02),
        'norm_g': 1.0 + nrm(ks[6], (DEPTH, 3, d), 0.05),
        'ffn_w_gu': nrm(ks[7], (DEPTH, 2, d, 2 * D_FF), d ** -0.5),
        'ffn_w_down': nrm(ks[8], (DEPTH, 2, D_FF, d), D_FF ** -0.5),
        'ev_w_in': nrm(ks[9], (N_EVEN, d, EVEN_IN), d ** -0.5),
        'ev_w_out': nrm(ks[10], (N_EVEN, EVEN_MIX, d), EVEN_MIX ** -0.5),
        'a_q_gain': 1.0 + nrm(ks[11], (N_EVEN, HEAD_DIM), 0.05),
        'a_k_gain': 1.0 + nrm(ks[12], (N_EVEN, HEAD_DIM), 0.05),
        'a_sink': nrm(ks[13], (N_EVEN, A_HEADS), 0.5),
        'b_v_gain': 1.0 + nrm(ks[14], (N_EVEN, B_W), 0.05),
        'b_ws': nrm(ks[15], (N_EVEN, B_GROUPS, B_CHUNK, B_CHUNK), B_CHUNK ** -0.5),
        'b_bias': 1.0 + nrm(ks[16], (N_EVEN, B_GROUPS, B_CHUNK), 0.1),
        'od_w_in': nrm(ks[17], (N_ODD, d, ODD_IN), d ** -0.5),
        'od_w_out': nrm(ks[18], (N_ODD, ODD_MIX, d), ODD_MIX ** -0.5),
        'c_w_pool': nrm(ks[19], (N_ODD, C_GROUPS, C_GROUP_W, C_GROUP_W), C_GROUP_W ** -0.5),
        'c_scale': 1.0 + nrm(ks[20], (N_ODD, C_W), 0.1),
        'd_q_gain': 1.0 + nrm(ks[21], (N_ODD, HEAD_DIM), 0.05),
        'd_k_gain': 1.0 + nrm(ks[22], (N_ODD, HEAD_DIM), 0.05),
        'd_rpb': nrm(ks[23], (N_ODD, D_HEADS, 2 * NA_ROWS - 1, 2 * NA_COLS - 1), 0.5),
    }


def reference(x, c, ctx, c_ctx, ada_w, ada_b, norm_g, ffn_w_gu, ffn_w_down,
              ev_w_in, ev_w_out, a_q_gain, a_k_gain, a_sink, b_v_gain, b_ws, b_bias,
              od_w_in, od_w_out, c_w_pool, c_scale, d_q_gain, d_k_gain, d_rpb):
    ang_r, ang_c = axial_rope_angles(x.shape[1])
    sc = jax.nn.silu(c)
    scc = jax.nn.silu(c_ctx)
    h, hc = x, ctx
    for li in range(DEPTH):
        need_ctx = li < DEPTH - 1
        m = jnp.split((sc @ ada_w[li] + ada_b[li])[:, None, :], N_MOD, axis=-1)
        mc = jnp.split(scc @ ada_w[li] + ada_b[li], N_MOD, axis=-1)
        h = half_ffn(h, norm_g[li, 0], m[0], m[1], m[2], ffn_w_gu[li, 0], ffn_w_down[li, 0])
        hc = half_ffn(hc, norm_g[li, 0], mc[0], mc[1], mc[2], ffn_w_gu[li, 0], ffn_w_down[li, 0])
        z = norm_mod(h, norm_g[li, 1], m[3], m[4])
        zc = norm_mod(hc, norm_g[li, 1], mc[3], mc[4])
        if li % 2 == 0:
            e = li // 2
            y, yc = even_mixer(z, zc, ev_w_in[e], ev_w_out[e], a_q_gain[e], a_k_gain[e], a_sink[e],
                               b_v_gain[e], b_ws[e], b_bias[e], ang_r, ang_c, need_ctx)
        else:
            o = li // 2
            y, yc = odd_mixer(z, zc, od_w_in[o], od_w_out[o], c_w_pool[o], c_scale[o],
                              d_q_gain[o], d_k_gain[o], d_rpb[o], need_ctx)
        h = h + m[5] * y
        h = half_ffn(h, norm_g[li, 2], m[6], m[7], m[8], ffn_w_gu[li, 1], ffn_w_down[li, 1])
        if need_ctx:
            hc = hc + mc[5] * yc
            hc = half_ffn(hc, norm_g[li, 2], mc[6], mc[7], mc[8], ffn_w_gu[li, 1], ffn_w_down[li, 1])
    return h
```

```python
import functools

import numpy as np
import jax
import jax.numpy as jnp
from jax import lax
from jax.experimental import pallas as pl
from jax.experimental.pallas import tpu as pltpu

F32 = jnp.float32
BF16 = jnp.bfloat16

D_MODEL = 1024
GRID_W = 64
HEAD_DIM = 64
MIX_HALF = D_MODEL // 2
N_MOD = 9
D_FF = ((8 * D_MODEL // 3 + 127) // 128) * 128
EPS = 1e-6
ROPE_THETA = 10000.0
NEG = -1e30
N_HEADS = MIX_HALF // HEAD_DIM
A_KV_HEADS = N_HEADS // 4
A_KV = A_KV_HEADS * HEAD_DIM
WINDOW = 128
A_BLOCK = 128
B_CHUNK = 128
EVEN_IN = MIX_HALF + 2 * A_KV + 2 * MIX_HALF
POOL_WINDOWS = (2, 4, 8, 16)
C_GROUP_W = MIX_HALF // len(POOL_WINDOWS)
NA_ROWS = 8
NA_COLS = 16
ODD_IN = 4 * MIX_HALF

LANES = 128
MXU_DIM = 256
VMEM_LIMIT = 56 * 1024 * 1024

TOK_TILE = 512
NA_QROWS = 8
NA_KROWS = 16
NA_KCOLS = 2 * NA_COLS
NA_KSTART = tuple(int(v) for v in np.clip(np.arange(GRID_W // NA_COLS) * NA_COLS - NA_COLS // 2,
                                         0, GRID_W - NA_KCOLS))
NA_COLVAR = (0, 1, 1, 2)
FFN_CHUNKS = (512, 512, 512, 512, 512, 256)
assert sum(FFN_CHUNKS) == D_FF

A_HEAD_PERM = (0, 4, 1, 5, 2, 6, 3, 7)


def _params(*sem):
    return pltpu.CompilerParams(dimension_semantics=sem, vmem_limit_bytes=VMEM_LIMIT)


def _resident(shape, index_map):
    return pl.BlockSpec(shape, index_map, pipeline_mode=pl.Buffered(1))


def _norm_mod(h, g, shift, scale):
    y = h * lax.rsqrt(jnp.mean(h * h, axis=-1, keepdims=True) + EPS)
    return (y * g) * (1.0 + scale) + shift


def _dot(a, b):
    return jnp.dot(a, b, preferred_element_type=F32)


def _dot_nt(a, b):
    return lax.dot_general(a, b, (((1,), (1,)), ((), ())), preferred_element_type=F32)


def _head_rms(x, gain, ones_bd):
    w = x.shape[1]
    step = min(w, MXU_DIM)
    ones = ones_bd[:step, :step]
    parts = []
    for c in range(0, w, step):
        xs = x[:, c:c + step]
        x2 = xs * xs
        hi = x2.astype(BF16)
        lo = (x2 - hi.astype(F32)).astype(BF16)
        ss = _dot(hi, ones) + _dot(lo, ones)
        parts.append(xs * lax.rsqrt(ss / HEAD_DIM + EPS))
    y = parts[0] if len(parts) == 1 else jnp.concatenate(parts, axis=1)
    return y * gain


def _rope(x, cos, sin_signed):
    lane = lax.broadcasted_iota(jnp.int32, (1, LANES), 1)
    first = (lane & 16) == 0
    parts = []
    for c in range(0, x.shape[1], LANES):
        xb = x[:, c:c + LANES]
        partner = jnp.where(first, pltpu.roll(xb, LANES - 16, 1), pltpu.roll(xb, 16, 1))
        parts.append(xb * cos + partner * sin_signed)
    return parts[0] if len(parts) == 1 else jnp.concatenate(parts, axis=1)


def _softmax_pv(s, sink, vcat):
    m = jnp.max(s, axis=-1, keepdims=True)
    if sink is not None:
        m = jnp.maximum(m, sink)
    p = jnp.exp(s - m)
    den = jnp.sum(p, axis=-1, keepdims=True)
    if sink is not None:
        den = den + jnp.exp(sink - m)
    return _dot(p.astype(BF16), vcat) / den


def _mod_kernel(c_ref, w_ref, b_ref, o_ref):
    sc = jax.nn.silu(c_ref[...])
    o_ref[...] = _dot(sc, w_ref[...]) + b_ref[...]


def _modulation(cvec, ada_w, ada_b):
    depth, d, n = ada_w.shape
    r = cvec.shape[0]
    tn = 1024
    out = pl.pallas_call(
        _mod_kernel,
        grid=(depth, n // tn),
        in_specs=[
            pl.BlockSpec((r, d), lambda l, j: (0, 0)),
            pl.BlockSpec((None, d, tn), lambda l, j: (l, 0, j)),
            pl.BlockSpec((None, 1, tn), lambda l, j: (l, 0, j)),
        ],
        out_specs=pl.BlockSpec((None, r, tn), lambda l, j: (l, 0, j)),
        out_shape=jax.ShapeDtypeStruct((depth, r, n), F32),
        compiler_params=_params("parallel", "parallel"),
        name="adaln_mod",
    )(cvec, ada_w, ada_b.reshape(depth, 1, n))
    return out.reshape(depth, r, N_MOD, d)


def _mod_spec(per_batch):
    if per_batch:
        return pl.BlockSpec((None, N_MOD, D_MODEL), lambda b, i: (b, 0, 0))
    return None


def _ffn_kernel(h_ref, mod_ref, g_ref, wgu_ref, wd_ref, o_ref, *, mrow):
    h = h_ref[...]
    z = _norm_mod(h, g_ref[...], mod_ref[mrow:mrow + 1, :], mod_ref[mrow + 1:mrow + 2, :]).astype(BF16)
    acc = None
    off = 0
    for ch in FFN_CHUNKS:
        gt = _dot(z, wgu_ref[:, off:off + ch])
        up = _dot(z, wgu_ref[:, D_FF + off:D_FF + off + ch])
        a = (jax.nn.silu(gt) * up).astype(BF16)
        part = _dot(a, wd_ref[off:off + ch, :])
        acc = part if acc is None else acc + part
        off += ch
    o_ref[...] = h + (0.5 * mod_ref[mrow + 2:mrow + 3, :]) * acc


def _half_ffn(h, mod, mod_row_of_batch, g, wgu, wd, mrow):
    b, s, d = h.shape
    tm = min(TOK_TILE, s)
    return pl.pallas_call(
        functools.partial(_ffn_kernel, mrow=mrow),
        grid=(b, s // tm),
        in_specs=[
            pl.BlockSpec((None, tm, d), lambda bi, i: (bi, i, 0)),
            pl.BlockSpec((None, N_MOD, d), lambda bi, i: (mod_row_of_batch(bi), 0, 0)),
            _resident((1, d), lambda bi, i: (0, 0)),
            _resident((d, 2 * D_FF), lambda bi, i: (0, 0)),
            _resident((D_FF, d), lambda bi, i: (0, 0)),
        ],
        out_specs=pl.BlockSpec((None, tm, d), lambda bi, i: (bi, i, 0)),
        out_shape=jax.ShapeDtypeStruct((b, s, d), F32),
        compiler_params=_params("parallel", "parallel"),
        name="half_ffn",
    )(h, mod, g.reshape(1, d), wgu, wd)


def _even_inproj_kernel(h_ref, mod_ref, g_ref, win_ref, qg_ref, kg_ref, vg_ref, cos_ref, sin_ref,
                        ones_ref, q_ref, k_ref, v_ref, u_ref, vv_ref):
    z = _norm_mod(h_ref[...], g_ref[...], mod_ref[3:4, :], mod_ref[4:5, :]).astype(BF16)
    ones = ones_ref[...]
    cos = cos_ref[...]
    sin = sin_ref[...]
    c0 = 0
    q = _dot(z, win_ref[:, c0:c0 + MIX_HALF])
    q = _rope(_head_rms(q, qg_ref[...], ones), cos, sin)
    q_ref[...] = (q * HEAD_DIM ** -0.5).astype(BF16)
    c0 += MIX_HALF
    k = _dot(z, win_ref[:, c0:c0 + A_KV])
    k_ref[...] = _rope(_head_rms(k, kg_ref[...], ones), cos, sin).astype(BF16)
    c0 += A_KV
    v_ref[...] = _dot(z, win_ref[:, c0:c0 + A_KV]).astype(BF16)
    c0 += A_KV
    u_ref[...] = jax.nn.gelu(_dot(z, win_ref[:, c0:c0 + MIX_HALF]))
    c0 += MIX_HALF
    bv = jax.nn.gelu(_dot(z, win_ref[:, c0:c0 + MIX_HALF]))
    vv_ref[...] = _head_rms(bv, vg_ref[...], ones).astype(BF16)


def _even_inproj(h, mod, mod_row_of_batch, g, win, qg, kg, vg, cos, sin, ones):
    b, s, d = h.shape
    tm = min(TOK_TILE, s)
    tok = lambda w: pl.BlockSpec((None, tm, w), lambda bi, i: (bi, i, 0))
    const = lambda shape: _resident(shape, lambda bi, i: (0,) * len(shape))
    return pl.pallas_call(
        _even_inproj_kernel,
        grid=(b, s // tm),
        in_specs=[
            tok(d),
            pl.BlockSpec((None, N_MOD, d), lambda bi, i: (mod_row_of_batch(bi), 0, 0)),
            const((1, d)), const((d, EVEN_IN)), const((1, MIX_HALF)), const((1, A_KV)),
            const((1, MIX_HALF)),
            pl.BlockSpec((tm, LANES), lambda bi, i: (i, 0)),
            pl.BlockSpec((tm, LANES), lambda bi, i: (i, 0)),
            const((MXU_DIM, MXU_DIM)),
        ],
        out_specs=[tok(MIX_HALF), tok(A_KV), tok(A_KV), tok(MIX_HALF), tok(MIX_HALF)],
        out_shape=[
            jax.ShapeDtypeStruct((b, s, MIX_HALF), BF16),
            jax.ShapeDtypeStruct((b, s, A_KV), BF16),
            jax.ShapeDtypeStruct((b, s, A_KV), BF16),
            jax.ShapeDtypeStruct((b, s, MIX_HALF), F32),
            jax.ShapeDtypeStruct((b, s, MIX_HALF), BF16),
        ],
        compiler_params=_params("parallel", "parallel"),
        name="even_inproj",
    )(h, mod, g.reshape(1, d), win, qg, kg, vg, cos, sin, ones)


def _even_mixer_kernel(*refs, local, seq):
    if local:
        (q_ref, k_ref, v_ref, kc_ref, vc_ref, sink_ref, u_ref, vv_ref, ws_ref, gb_ref, wout_ref,
         mod_ref, h_ref, o_ref, mix_ref) = refs
    else:
        (q_ref, kc_ref, vc_ref, sink_ref, u_ref, vv_ref, ws_ref, gb_ref, wout_ref,
         mod_ref, h_ref, o_ref, mix_ref) = refs
    tq = q_ref.shape[0]
    nblk = tq // A_BLOCK
    nloc = 3 * A_BLOCK
    lane = lax.broadcasted_iota(jnp.int32, (1, LANES), 1)
    low = lane < HEAD_DIM
    kc = kc_ref[...]
    vc = vc_ref[...]
    lc = kc.shape[0]
    groups = N_HEADS // A_KV_HEADS
    for blk in range(nblk):
        r0 = blk * A_BLOCK
        qb = q_ref[r0:r0 + A_BLOCK, :]
        if local:
            n = pl.program_id(1) * nblk + blk
            start = pl.multiple_of(jnp.clip((n - 1) * A_BLOCK, 0, seq - nloc), A_BLOCK)
            kcat = jnp.concatenate([k_ref[pl.ds(start, nloc), :], kc], axis=0)
            vcat = jnp.concatenate([v_ref[pl.ds(start, nloc), :], vc], axis=0)
            qpos = n * A_BLOCK + lax.broadcasted_iota(jnp.int32, (A_BLOCK, 1), 0)
            kpos = start + lax.broadcasted_iota(jnp.int32, (1, nloc), 1)
            mask = jnp.where(jnp.abs(qpos - kpos) <= WINDOW, 0.0, NEG)
            mask = jnp.concatenate([mask, jnp.zeros((A_BLOCK, lc), F32)], axis=1)
            mask = jnp.concatenate([mask] * groups, axis=0)
        else:
            kcat, vcat, mask = kc, vc, None
        outs = []
        for j in range(A_KV_HEADS):
            keep = low if j == 0 else jnp.logical_not(low)
            q4 = jnp.concatenate(
                [jnp.where(keep, qb[:, p * LANES:(p + 1) * LANES], jnp.zeros((), BF16))
                 for p in range(groups)], axis=0)
            s = _dot_nt(q4, kcat)
            if mask is not None:
                s = s + mask
            outs.append(_softmax_pv(s, sink_ref[j], vcat))
        for p in range(groups):
            rows = slice(p * A_BLOCK, (p + 1) * A_BLOCK)
            o = jnp.where(low, outs[0][rows], outs[1][rows])
            mix_ref[r0:r0 + A_BLOCK, p * LANES:(p + 1) * LANES] = o.astype(BF16)
        vvb = vv_ref[r0:r0 + B_CHUNK, :]
        for p in range(MIX_HALF // LANES):
            cols = slice(p * LANES, (p + 1) * LANES)
            vsl = vvb[:, cols]
            mixed = jnp.where(low, _dot(ws_ref[2 * p], vsl), _dot(ws_ref[2 * p + 1], vsl)) + gb_ref[:, cols]
            mix_ref[r0:r0 + B_CHUNK, MIX_HALF + p * LANES:MIX_HALF + (p + 1) * LANES] = (
                u_ref[r0:r0 + B_CHUNK, cols] * mixed).astype(BF16)
    y = _dot(mix_ref[...], wout_ref[...])
    o_ref[...] = h_ref[...] + mod_ref[5:6, :] * y


def _even_mixer(q, k, v, kc, vc, sink_cols, u, vv, ws, gb, wout, mod, mod_row_of_batch, h, local):
    b, s, d = h.shape
    lc = kc.shape[1]
    tq = min(TOK_TILE, s)
    tok = lambda w: pl.BlockSpec((None, tq, w), lambda bi, i: (bi, i, 0))
    per_batch = lambda rows, w: pl.BlockSpec((None, rows, w), lambda bi, i: (bi, 0, 0))
    const = lambda shape: _resident(shape, lambda bi, i: (0,) * len(shape))
    in_specs = [tok(MIX_HALF)]
    args = [q]
    if local:
        in_specs += [per_batch(s, A_KV), per_batch(s, A_KV)]
        args += [k, v]
    in_specs += [
        per_batch(lc, A_KV), per_batch(lc, A_KV),
        const(sink_cols.shape),
        tok(MIX_HALF), tok(MIX_HALF),
        const(ws.shape), const(gb.shape), const((d, d)),
        pl.BlockSpec((None, N_MOD, d), lambda bi, i: (mod_row_of_batch(bi), 0, 0)),
        tok(d),
    ]
    args += [kc, vc, sink_cols, u, vv, ws, gb, wout, mod, h]
    return pl.pallas_call(
        functools.partial(_even_mixer_kernel, local=local, seq=s),
        grid=(b, s // tq),
        in_specs=in_specs,
        out_specs=tok(d),
        out_shape=jax.ShapeDtypeStruct((b, s, d), F32),
        scratch_shapes=[pltpu.VMEM((tq, d), BF16)],
        compiler_params=_params("parallel", "parallel"),
        name="even_mixer" if local else "even_mixer_ctx",
    )(*args)


def _odd_inproj_kernel(h_ref, mod_ref, g_ref, win_ref, qg_ref, kg_ref, ones_ref, *out_refs, kv_only):
    z = _norm_mod(h_ref[...], g_ref[...], mod_ref[3:4, :], mod_ref[4:5, :]).astype(BF16)
    ones = ones_ref[...]
    w = MIX_HALF
    if kv_only:
        k_ref, v_ref = out_refs
    else:
        xp_ref, q_ref, k_ref, v_ref = out_refs
        xp_ref[...] = _dot(z, win_ref[:, 0:w])
        q = _head_rms(_dot(z, win_ref[:, w:2 * w]), qg_ref[...], ones)
        q_ref[...] = (q * HEAD_DIM ** -0.5).astype(BF16)
    k_ref[...] = _head_rms(_dot(z, win_ref[:, 2 * w:3 * w]), kg_ref[...], ones).astype(BF16)
    v_ref[...] = _dot(z, win_ref[:, 3 * w:4 * w]).astype(BF16)


def _odd_inproj(h, mod, mod_row_of_batch, g, win, qg, kg, ones, kv_only):
    b, s, d = h.shape
    tm = min(TOK_TILE, s)
    tok = lambda w: pl.BlockSpec((None, tm, w), lambda bi, i: (bi, i, 0))
    const = lambda shape: _resident(shape, lambda bi, i: (0,) * len(shape))
    bf = jax.ShapeDtypeStruct((b, s, MIX_HALF), BF16)
    if kv_only:
        out_specs, out_shape = [tok(MIX_HALF)] * 2, [bf, bf]
    else:
        out_specs = [tok(MIX_HALF)] * 4
        out_shape = [jax.ShapeDtypeStruct((b, s, MIX_HALF), F32), bf, bf, bf]
    return pl.pallas_call(
        functools.partial(_odd_inproj_kernel, kv_only=kv_only),
        grid=(b, s // tm),
        in_specs=[
            tok(d),
            pl.BlockSpec((None, N_MOD, d), lambda bi, i: (mod_row_of_batch(bi), 0, 0)),
            const((1, d)), const((d, ODD_IN)), const((1, MIX_HALF)), const((1, MIX_HALF)),
            const((MXU_DIM, MXU_DIM)),
        ],
        out_specs=out_specs,
        out_shape=out_shape,
        compiler_params=_params("parallel", "parallel"),
        name="odd_inproj_kv" if kv_only else "odd_inproj",
    )(h, mod, g.reshape(1, d), win, qg, kg, ones)


def _na_tables(rows):
    nrb = rows // NA_QROWS
    rv_blocks = (0, nrb // 2, nrb - 1)
    ridx = np.zeros((3, 3, NA_QROWS, NA_COLS, NA_KROWS, NA_KCOLS), np.int32)
    cidx = np.zeros_like(ridx)
    valid = np.zeros(ridx.shape, bool)
    for rv, a in enumerate(rv_blocks):
        ks = int(np.clip(NA_QROWS * a - NA_ROWS // 2, 0, rows - NA_KROWS))
        for cv, m in enumerate((0, 1, 3)):
            r = NA_QROWS * a + np.arange(NA_QROWS)[:, None, None, None]
            qc = NA_COLS * m + np.arange(NA_COLS)[None, :, None, None]
            kr = ks + np.arange(NA_KROWS)[None, None, :, None]
            kcol = NA_KSTART[m] + np.arange(NA_KCOLS)[None, None, None, :]
            r0 = np.clip(r - NA_ROWS // 2, 0, rows - NA_ROWS)
            qstart = np.clip(qc - NA_COLS // 2, 0, GRID_W - NA_COLS)
            ok = (kr >= r0) & (kr < r0 + NA_ROWS) & (kcol >= qstart) & (kcol < qstart + NA_COLS)
            ri = np.clip(kr - r + NA_ROWS - 1, 0, 2 * NA_ROWS - 2)
            ci = np.clip(kcol - qc, 1 - NA_COLS, NA_COLS - 1) + NA_COLS - 1
            shape = ridx.shape[2:]
            ridx[rv, cv] = np.broadcast_to(ri, shape)
            cidx[rv, cv] = np.broadcast_to(ci, shape)
            valid[rv, cv] = np.broadcast_to(ok, shape)
    nq = NA_QROWS * NA_COLS
    nk = NA_KROWS * NA_KCOLS
    return ridx.reshape(3, 3, nq, nk), cidx.reshape(3, 3, nq, nk), valid.reshape(3, 3, nq, nk)


def _odd_mixer_kernel(xp_ref, xprev_ref, xnext_ref, q_ref, k_ref, v_ref, kc_ref, vc_ref, bias_ref,
                      wpool_ref, cs_ref, wout_ref, mod_ref, h_ref, o_ref, mix_ref, xw_ref, *, seq):
    a = pl.program_id(1)
    last = pl.num_programs(1) - 1
    tq = q_ref.shape[0]
    halo = xprev_ref.shape[0]
    lane = lax.broadcasted_iota(jnp.int32, (1, LANES), 1)
    low = lane < HEAD_DIM

    xw_ref[0:halo, :] = jnp.where(a == 0, 0.0, xprev_ref[...])
    xw_ref[halo:halo + tq, :] = xp_ref[...]
    xw_ref[halo + tq:halo + tq + halo, :] = jnp.where(a == last, 0.0, xnext_ref[...])
    t = a * tq + lax.broadcasted_iota(jnp.int32, (tq, 1), 0)
    for gi, w in enumerate(POOL_WINDOWS):
        cols = slice(gi * C_GROUP_W, (gi + 1) * C_GROUP_W)
        acc = None
        for dlt in range(-(w // 2), w - w // 2):
            term = xw_ref[halo + dlt:halo + dlt + tq, cols]
            acc = term if acc is None else acc + term
        cnt = (jnp.minimum(t + (w - w // 2), seq) - jnp.maximum(t - w // 2, 0)).astype(F32)
        pooled = acc / cnt - xp_ref[:, cols]
        y = _dot(pooled.astype(BF16), wpool_ref[gi]) * cs_ref[:, cols]
        mix_ref[:, cols] = y.astype(BF16)

    rows = seq // GRID_W
    ks = jnp.clip(NA_QROWS * a - NA_ROWS // 2, 0, rows - NA_KROWS)
    tok0 = pl.multiple_of(ks * GRID_W, GRID_W)
    nkw = NA_KROWS * GRID_W
    for p in range(MIX_HALF // LANES):
        cols = slice(p * LANES, (p + 1) * LANES)
        kwin = k_ref[pl.ds(tok0, nkw), cols].astype(F32).reshape(NA_KROWS, GRID_W, LANES)
        vwin = v_ref[pl.ds(tok0, nkw), cols].astype(F32).reshape(NA_KROWS, GRID_W, LANES)
        kcp = kc_ref[:, cols]
        vcp = vc_ref[:, cols]
        for m in range(GRID_W // NA_COLS):
            kst = NA_KSTART[m]
            nloc = NA_KROWS * NA_KCOLS
            kl = kwin[:, kst:kst + NA_KCOLS, :].reshape(nloc, LANES).astype(BF16)
            vl = vwin[:, kst:kst + NA_KCOLS, :].reshape(nloc, LANES).astype(BF16)
            kcat = jnp.concatenate([kl, kcp], axis=0)
            vcat = jnp.concatenate([vl, vcp], axis=0)
            qm = jnp.concatenate(
                [q_ref[r * GRID_W + m * NA_COLS:r * GRID_W + (m + 1) * NA_COLS, cols]
                 for r in range(NA_QROWS)], axis=0)
            outs = []
            for hh in range(2):
                keep = low if hh == 0 else jnp.logical_not(low)
                s = _dot_nt(jnp.where(keep, qm, jnp.zeros((), BF16)), kcat)
                s = jnp.concatenate([s[:, :nloc] + bias_ref[NA_COLVAR[m], 2 * p + hh], s[:, nloc:]], axis=1)
                outs.append(_softmax_pv(s, None, vcat))
            o = jnp.where(low, outs[0], outs[1]).astype(BF16)
            for r in range(NA_QROWS):
                mix_ref[r * GRID_W + m * NA_COLS:r * GRID_W + (m + 1) * NA_COLS,
                        MIX_HALF + p * LANES:MIX_HALF + (p + 1) * LANES] = o[r * NA_COLS:(r + 1) * NA_COLS]
    y = _dot(mix_ref[...], wout_ref[...])
    o_ref[...] = h_ref[...] + mod_ref[5:6, :] * y


def _odd_mixer(xp, q, k, v, kc, vc, bias_tbl, wpool, cscale, wout, mod, h):
    b, s, d = h.shape
    lc = kc.shape[1]
    tq = NA_QROWS * GRID_W
    nrb = s // tq
    halo = 8
    hb = tq // halo
    tok = lambda w: pl.BlockSpec((None, tq, w), lambda bi, i: (bi, i, 0))
    per_batch = lambda rows, w: _resident((None, rows, w), lambda bi, i: (bi, 0, 0))
    const = lambda shape: _resident(shape, lambda bi, i: (0,) * len(shape))

    def row_variant(i):
        return jnp.where(i == 0, 0, jnp.where(i == nrb - 1, 2, 1))

    return pl.pallas_call(
        functools.partial(_odd_mixer_kernel, seq=s),
        grid=(b, nrb),
        in_specs=[
            tok(MIX_HALF),
            pl.BlockSpec((None, halo, MIX_HALF), lambda bi, i: (bi, jnp.maximum(i * hb - 1, 0), 0)),
            pl.BlockSpec((None, halo, MIX_HALF), lambda bi, i: (bi, jnp.minimum((i + 1) * hb, s // halo - 1), 0)),
            tok(MIX_HALF),
            per_batch(s, MIX_HALF), per_batch(s, MIX_HALF),
            per_batch(lc, MIX_HALF), per_batch(lc, MIX_HALF),
            pl.BlockSpec((None,) + bias_tbl.shape[1:], lambda bi, i: (row_variant(i), 0, 0, 0, 0)),
            const(wpool.shape), const(cscale.shape), const((d, d)),
            pl.BlockSpec((None, N_MOD, d), lambda bi, i: (bi, 0, 0)),
            tok(d),
        ],
        out_specs=tok(d),
        out_shape=jax.ShapeDtypeStruct((b, s, d), F32),
        scratch_shapes=[pltpu.VMEM((tq, d), BF16), pltpu.VMEM((tq + 2 * halo, MIX_HALF), F32)],
        compiler_params=_params("parallel", "arbitrary"),
        name="odd_mixer",
    )(xp, xp, xp, q, k, v, kc, vc, bias_tbl, wpool, cscale, wout, mod, h)


def _rope_tables(s):
    t = jnp.arange(s)
    row = (t // GRID_W).astype(F32)
    col = (t % GRID_W).astype(F32)
    m = HEAD_DIM // 4
    inv = 1.0 / (ROPE_THETA ** (jnp.arange(m, dtype=F32) / m))
    ang_r = row[:, None] * inv[None, :]
    ang_c = col[:, None] * inv[None, :]
    cos_h = jnp.concatenate([jnp.cos(ang_r)] * 2 + [jnp.cos(ang_c)] * 2, axis=-1)
    sin_h = jnp.concatenate([-jnp.sin(ang_r), jnp.sin(ang_r), -jnp.sin(ang_c), jnp.sin(ang_c)], axis=-1)
    reps = LANES // HEAD_DIM
    return jnp.tile(cos_h, (1, reps)), jnp.tile(sin_h, (1, reps))


def _block_diag_ones():
    idx = np.arange(MXU_DIM) // HEAD_DIM
    return jnp.asarray(idx[:, None] == idx[None, :], dtype=BF16)


def kernel(x, c, ctx, c_ctx, ada_w, ada_b, norm_g, ffn_w_gu, ffn_w_down, ev_w_in, ev_w_out,
           a_q_gain, a_k_gain, a_sink, b_v_gain, b_ws, b_bias, od_w_in, od_w_out, c_w_pool,
           c_scale, d_q_gain, d_k_gain, d_rpb):
    b, s, d = x.shape
    lc = ctx.shape[1]
    depth = ada_w.shape[0]
    assert d == D_MODEL and depth == 2 and s % TOK_TILE == 0 and s // GRID_W >= NA_KROWS
    assert lc % B_CHUNK == 0 and b <= 8

    mod_rows = 16
    cvec = jnp.zeros((mod_rows, d), F32).at[:b].set(c).at[b].set(c_ctx)
    mod = _modulation(cvec, ada_w, ada_b)
    lat_row = lambda bi: bi
    ctx_row = lambda bi: b

    ones = _block_diag_ones()
    cos, sin = _rope_tables(s)
    cos_id = jnp.ones((lc, LANES), F32)
    sin_id = jnp.zeros((lc, LANES), F32)
    tile_heads = lambda gain, n: jnp.tile(gain, n).reshape(1, n * HEAD_DIM)

    h, hc = x, ctx
    li, e = 0, 0
    wgu = ffn_w_gu[li].astype(BF16)
    wd = ffn_w_down[li].astype(BF16)
    h = _half_ffn(h, mod[li], lat_row, norm_g[li, 0], wgu[0], wd[0], 0)
    hc = _half_ffn(hc, mod[li], ctx_row, norm_g[li, 0], wgu[0], wd[0], 0)
    qperm = np.concatenate([np.arange(HEAD_DIM) + HEAD_DIM * hd for hd in A_HEAD_PERM])
    win = jnp.concatenate([ev_w_in[e][:, qperm], ev_w_in[e][:, MIX_HALF:]], axis=1).astype(BF16)
    wout = jnp.concatenate([ev_w_out[e][qperm], ev_w_out[e][MIX_HALF:]], axis=0).astype(BF16)
    qg = tile_heads(a_q_gain[e], N_HEADS)
    kg = tile_heads(a_k_gain[e], A_KV_HEADS)
    vg = b_v_gain[e].reshape(1, MIX_HALF)
    q, k, v, u, vv = _even_inproj(h, mod[li], lat_row, norm_g[li, 1], win, qg, kg, vg, cos, sin, ones)
    qc, kc, vc, uc, vvc = _even_inproj(hc, mod[li], ctx_row, norm_g[li, 1], win, qg, kg, vg,
                                       cos_id, sin_id, ones)
    sink_cols = jnp.repeat(a_sink[e].reshape(A_KV_HEADS, N_HEADS // A_KV_HEADS), A_BLOCK, axis=1)[..., None]
    ws = b_ws[e].astype(BF16)
    gb = jnp.repeat(b_bias[e].T, HEAD_DIM, axis=1)
    h = _even_mixer(q, k, v, kc, vc, sink_cols, u, vv, ws, gb, wout, mod[li], lat_row, h, True)
    hc = _even_mixer(qc, None, None, kc, vc, sink_cols, uc, vvc, ws, gb, wout, mod[li], ctx_row, hc, False)
    h = _half_ffn(h, mod[li], lat_row, norm_g[li, 2], wgu[1], wd[1], 6)
    hc = _half_ffn(hc, mod[li], ctx_row, norm_g[li, 2], wgu[1], wd[1], 6)

    li, o = 1, 0
    wgu = ffn_w_gu[li].astype(BF16)
    wd = ffn_w_down[li].astype(BF16)
    h = _half_ffn(h, mod[li], lat_row, norm_g[li, 0], wgu[0], wd[0], 0)
    hc = _half_ffn(hc, mod[li], ctx_row, norm_g[li, 0], wgu[0], wd[0], 0)
    win = od_w_in[o].astype(BF16)
    wout = od_w_out[o].astype(BF16)
    qg = tile_heads(d_q_gain[o], N_HEADS)
    kg = tile_heads(d_k_gain[o], N_HEADS)
    xp, q, k, v = _odd_inproj(h, mod[li], lat_row, norm_g[li, 1], win, qg, kg, ones, False)
    kc, vc = _odd_inproj(hc, mod[li], ctx_row, norm_g[li, 1], win, qg, kg, ones, True)
    ridx, cidx, valid = _na_tables(s // GRID_W)
    bias_tbl = jnp.where(valid[:, :, None], jnp.transpose(d_rpb[o][:, ridx, cidx], (1, 2, 0, 3, 4)), NEG)
    h = _odd_mixer(xp, q, k, v, kc, vc, bias_tbl, c_w_pool[o].astype(BF16), c_scale[o].reshape(1, MIX_HALF),
                   wout, mod[li], h)
    h = _half_ffn(h, mod[li], lat_row, norm_g[li, 2], wgu[1], wd[1], 6)
    return h
```

```python
import functools

import numpy as np
import jax
import jax.numpy as jnp
from jax import lax
from jax.experimental import pallas as pl
from jax.experimental.pallas import tpu as pltpu

F32 = jnp.float32
BF16 = jnp.bfloat16

D_MODEL = 1024
GRID_W = 64
HEAD_DIM = 64
MIX_HALF = D_MODEL // 2
N_MOD = 9
D_FF = ((8 * D_MODEL // 3 + 127) // 128) * 128
EPS = 1e-6
ROPE_THETA = 10000.0
NEG = -1e30
N_HEADS = MIX_HALF // HEAD_DIM
A_KV_HEADS = N_HEADS // 4
A_KV = A_KV_HEADS * HEAD_DIM
WINDOW = 128
A_BLOCK = 128
B_CHUNK = 128
EVEN_IN = MIX_HALF + 2 * A_KV + 2 * MIX_HALF
POOL_WINDOWS = (2, 4, 8, 16)
C_GROUP_W = MIX_HALF // len(POOL_WINDOWS)
NA_ROWS = 8
NA_COLS = 16
ODD_IN = 4 * MIX_HALF

LANES = 128
MXU_DIM = 256
VMEM_LIMIT = 56 * 1024 * 1024

TOK_TILE = 512
NA_QROWS = 8
NA_KROWS = 16
NA_KCOLS = 2 * NA_COLS
NA_KSTART = tuple(int(v) for v in np.clip(np.arange(GRID_W // NA_COLS) * NA_COLS - NA_COLS // 2,
                                         0, GRID_W - NA_KCOLS))
NA_COLVAR = (0, 1, 1, 2)
FFN_CHUNKS = (512, 512, 512, 512, 512, 256)
assert sum(FFN_CHUNKS) == D_FF

A_HEAD_PERM = (0, 4, 1, 5, 2, 6, 3, 7)


def _params(*sem):
    return pltpu.CompilerParams(dimension_semantics=sem, vmem_limit_bytes=VMEM_LIMIT)


def _resident(shape, index_map):
    return pl.BlockSpec(shape, index_map, pipeline_mode=pl.Buffered(1))


def _norm_mod(h, g, shift, scale):
    y = h * lax.rsqrt(jnp.mean(h * h, axis=-1, keepdims=True) + EPS)
    return (y * g) * (1.0 + scale) + shift


def _dot(a, b):
    return jnp.dot(a, b, preferred_element_type=F32)


def _dot_nt(a, b):
    return lax.dot_general(a, b, (((1,), (1,)), ((), ())), preferred_element_type=F32)


def _head_rms(x, gain, ones_bd):
    w = x.shape[1]
    step = min(w, MXU_DIM)
    ones = ones_bd[:step, :step]
    parts = []
    for c in range(0, w, step):
        xs = x[:, c:c + step]
        x2 = xs * xs
        hi = x2.astype(BF16)
        lo = (x2 - hi.astype(F32)).astype(BF16)
        ss = _dot(hi, ones) + _dot(lo, ones)
        parts.append(xs * lax.rsqrt(ss / HEAD_DIM + EPS))
    y = parts[0] if len(parts) == 1 else jnp.concatenate(parts, axis=1)
    return y * gain


def _rope(x, cos, sin_signed):
    lane = lax.broadcasted_iota(jnp.int32, (1, LANES), 1)
    first = (lane & 16) == 0
    parts = []
    for c in range(0, x.shape[1], LANES):
        xb = x[:, c:c + LANES]
        partner = jnp.where(first, pltpu.roll(xb, LANES - 16, 1), pltpu.roll(xb, 16, 1))
        parts.append(xb * cos + partner * sin_signed)
    return parts[0] if len(parts) == 1 else jnp.concatenate(parts, axis=1)


def _softmax_pv(s, sink, vcat):
    m = jnp.max(s, axis=-1, keepdims=True)
    if sink is not None:
        m = jnp.maximum(m, sink)
    p = jnp.exp(s - m)
    den = jnp.sum(p, axis=-1, keepdims=True)
    if sink is not None:
        den = den + jnp.exp(sink - m)
    return _dot(p.astype(BF16), vcat) / den


def _mod_kernel(c_ref, w_ref, b_ref, o_ref):
    sc = jax.nn.silu(c_ref[...])
    o_ref[...] = _dot(sc, w_ref[...]) + b_ref[...]


def _modulation(cvec, ada_w, ada_b):
    depth, d, n = ada_w.shape
    r = cvec.shape[0]
    tn = 1024
    out = pl.pallas_call(
        _mod_kernel,
        grid=(depth, n // tn),
        in_specs=[
            pl.BlockSpec((r, d), lambda l, j: (0, 0)),
            pl.BlockSpec((None, d, tn), lambda l, j: (l, 0, j)),
            pl.BlockSpec((None, 1, tn), lambda l, j: (l, 0, j)),
        ],
        out_specs=pl.BlockSpec((None, r, tn), lambda l, j: (l, 0, j)),
        out_shape=jax.ShapeDtypeStruct((depth, r, n), F32),
        compiler_params=_params("parallel", "parallel"),
        name="adaln_mod",
    )(cvec, ada_w, ada_b.reshape(depth, 1, n))
    return out.reshape(depth, r, N_MOD, d)


def _ffn_kernel(h_ref, mod_ref, g_ref, wgu_ref, wd_ref, o_ref, *, mrow):
    h = h_ref[...]
    z = _norm_mod(h, g_ref[...], mod_ref[mrow:mrow + 1, :], mod_ref[mrow + 1:mrow + 2, :]).astype(BF16)
    acc = None
    off = 0
    for ch in FFN_CHUNKS:
        gt = _dot(z, wgu_ref[:, off:off + ch])
        up = _dot(z, wgu_ref[:, D_FF + off:D_FF + off + ch])
        a = (jax.nn.silu(gt) * up).astype(BF16)
        part = _dot(a, wd_ref[off:off + ch, :])
        acc = part if acc is None else acc + part
        off += ch
    o_ref[...] = h + (0.5 * mod_ref[mrow + 2:mrow + 3, :]) * acc


def _half_ffn(h, mod, mod_row_of_batch, g, wgu, wd, mrow):
    b, s, d = h.shape
    tm = min(TOK_TILE, s)
    return pl.pallas_call(
        functools.partial(_ffn_kernel, mrow=mrow),
        grid=(b, s // tm),
        in_specs=[
            pl.BlockSpec((None, tm, d), lambda bi, i: (bi, i, 0)),
            pl.BlockSpec((None, N_MOD, d), lambda bi, i: (mod_row_of_batch(bi), 0, 0)),
            _resident((1, d), lambda bi, i: (0, 0)),
            _resident((d, 2 * D_FF), lambda bi, i: (0, 0)),
            _resident((D_FF, d), lambda bi, i: (0, 0)),
        ],
        out_specs=pl.BlockSpec((None, tm, d), lambda bi, i: (bi, i, 0)),
        out_shape=jax.ShapeDtypeStruct((b, s, d), F32),
        compiler_params=_params("parallel", "parallel"),
        name="half_ffn",
    )(h, mod, g.reshape(1, d), wgu, wd)


def _even_inproj_kernel(h_ref, mod_ref, g_ref, win_ref, qg_ref, kg_ref, vg_ref, cos_ref, sin_ref,
                        ones_ref, q_ref, k_ref, v_ref, u_ref, vv_ref):
    z = _norm_mod(h_ref[...], g_ref[...], mod_ref[3:4, :], mod_ref[4:5, :]).astype(BF16)
    ones = ones_ref[...]
    cos = cos_ref[...]
    sin = sin_ref[...]
    c0 = 0
    q = _dot(z, win_ref[:, c0:c0 + MIX_HALF])
    q = _rope(_head_rms(q, qg_ref[...], ones), cos, sin)
    q_ref[...] = (q * HEAD_DIM ** -0.5).astype(BF16)
    c0 += MIX_HALF
    k = _dot(z, win_ref[:, c0:c0 + A_KV])
    k_ref[...] = _rope(_head_rms(k, kg_ref[...], ones), cos, sin).astype(BF16)
    c0 += A_KV
    v_ref[...] = _dot(z, win_ref[:, c0:c0 + A_KV]).astype(BF16)
    c0 += A_KV
    u_ref[...] = jax.nn.gelu(_dot(z, win_ref[:, c0:c0 + MIX_HALF]))
    c0 += MIX_HALF
    bv = jax.nn.gelu(_dot(z, win_ref[:, c0:c0 + MIX_HALF]))
    vv_ref[...] = _head_rms(bv, vg_ref[...], ones).astype(BF16)


def _even_inproj(h, mod, mod_row_of_batch, g, win, qg, kg, vg, cos, sin, ones):
    b, s, d = h.shape
    tm = min(TOK_TILE, s)
    tok = lambda w: pl.BlockSpec((None, tm, w), lambda bi, i: (bi, i, 0))
    const = lambda shape: _resident(shape, lambda bi, i: (0,) * len(shape))
    return pl.pallas_call(
        _even_inproj_kernel,
        grid=(b, s // tm),
        in_specs=[
            tok(d),
            pl.BlockSpec((None, N_MOD, d), lambda bi, i: (mod_row_of_batch(bi), 0, 0)),
            const((1, d)), const((d, EVEN_IN)), const((1, MIX_HALF)), const((1, A_KV)),
            const((1, MIX_HALF)),
            pl.BlockSpec((tm, LANES), lambda bi, i: (i, 0)),
            pl.BlockSpec((tm, LANES), lambda bi, i: (i, 0)),
            const((MXU_DIM, MXU_DIM)),
        ],
        out_specs=[tok(MIX_HALF), tok(A_KV), tok(A_KV), tok(MIX_HALF), tok(MIX_HALF)],
        out_shape=[
            jax.ShapeDtypeStruct((b, s, MIX_HALF), BF16),
            jax.ShapeDtypeStruct((b, s, A_KV), BF16),
            jax.ShapeDtypeStruct((b, s, A_KV), BF16),
            jax.ShapeDtypeStruct((b, s, MIX_HALF), F32),
            jax.ShapeDtypeStruct((b, s, MIX_HALF), BF16),
        ],
        compiler_params=_params("parallel", "parallel"),
        name="even_inproj",
    )(h, mod, g.reshape(1, d), win, qg, kg, vg, cos, sin, ones)


def _even_mixer_kernel(*refs, local, seq):
    if local:
        (q_ref, k_ref, v_ref, kc_ref, vc_ref, sink_ref, u_ref, vv_ref, ws_ref, gb_ref, wout_ref,
         mod_ref, h_ref, o_ref, mix_ref) = refs
    else:
        (q_ref, kc_ref, vc_ref, sink_ref, u_ref, vv_ref, ws_ref, gb_ref, wout_ref,
         mod_ref, h_ref, o_ref, mix_ref) = refs
    tq = q_ref.shape[0]
    nblk = tq // A_BLOCK
    nloc = 3 * A_BLOCK
    lane = lax.broadcasted_iota(jnp.int32, (1, LANES), 1)
    low = lane < HEAD_DIM
    kc = kc_ref[...]
    vc = vc_ref[...]
    lc = kc.shape[0]
    groups = N_HEADS // A_KV_HEADS
    for blk in range(nblk):
        r0 = blk * A_BLOCK
        qb = q_ref[r0:r0 + A_BLOCK, :]
        if local:
            n = pl.program_id(1) * nblk + blk
            start = pl.multiple_of(jnp.clip((n - 1) * A_BLOCK, 0, seq - nloc), A_BLOCK)
            kcat = jnp.concatenate([k_ref[pl.ds(start, nloc), :], kc], axis=0)
            vcat = jnp.concatenate([v_ref[pl.ds(start, nloc), :], vc], axis=0)
            qpos = n * A_BLOCK + lax.broadcasted_iota(jnp.int32, (A_BLOCK, 1), 0)
            kpos = start + lax.broadcasted_iota(jnp.int32, (1, nloc), 1)
            mask = jnp.where(jnp.abs(qpos - kpos) <= WINDOW, 0.0, NEG)
            mask = jnp.concatenate([mask, jnp.zeros((A_BLOCK, lc), F32)], axis=1)
            mask = jnp.concatenate([mask] * groups, axis=0)
        else:
            kcat, vcat, mask = kc, vc, None
        outs = []
        for j in range(A_KV_HEADS):
            keep = low if j == 0 else jnp.logical_not(low)
            q4 = jnp.concatenate(
                [jnp.where(keep, qb[:, p * LANES:(p + 1) * LANES], jnp.zeros((), BF16))
                 for p in range(groups)], axis=0)
            s = _dot_nt(q4, kcat)
            if mask is not None:
                s = s + mask
            outs.append(_softmax_pv(s, sink_ref[j], vcat))
        for p in range(groups):
            rows = slice(p * A_BLOCK, (p + 1) * A_BLOCK)
            o = jnp.where(low, outs[0][rows], outs[1][rows])
            mix_ref[r0:r0 + A_BLOCK, p * LANES:(p + 1) * LANES] = o.astype(BF16)
        vvb = vv_ref[r0:r0 + B_CHUNK, :]
        for p in range(MIX_HALF // LANES):
            cols = slice(p * LANES, (p + 1) * LANES)
            vsl = vvb[:, cols]
            mixed = jnp.where(low, _dot(ws_ref[2 * p], vsl), _dot(ws_ref[2 * p + 1], vsl)) + gb_ref[:, cols]
            mix_ref[r0:r0 + B_CHUNK, MIX_HALF + p * LANES:MIX_HALF + (p + 1) * LANES] = (
                u_ref[r0:r0 + B_CHUNK, cols] * mixed).astype(BF16)
    y = _dot(mix_ref[...], wout_ref[...])
    o_ref[...] = h_ref[...] + mod_ref[5:6, :] * y


def _even_mixer(q, k, v, kc, vc, sink_cols, u, vv, ws, gb, wout, mod, mod_row_of_batch, h, local):
    b, s, d = h.shape
    lc = kc.shape[1]
    tq = min(TOK_TILE, s)
    tok = lambda w: pl.BlockSpec((None, tq, w), lambda bi, i: (bi, i, 0))
    per_batch = lambda rows, w: pl.BlockSpec((None, rows, w), lambda bi, i: (bi, 0, 0))
    const = lambda shape: _resident(shape, lambda bi, i: (0,) * len(shape))
    in_specs = [tok(MIX_HALF)]
    args = [q]
    if local:
        in_specs += [per_batch(s, A_KV), per_batch(s, A_KV)]
        args += [k, v]
    in_specs += [
        per_batch(lc, A_KV), per_batch(lc, A_KV),
        const(sink_cols.shape),
        tok(MIX_HALF), tok(MIX_HALF),
        const(ws.shape), const(gb.shape), const((d, d)),
        pl.BlockSpec((None, N_MOD, d), lambda bi, i: (mod_row_of_batch(bi), 0, 0)),
        tok(d),
    ]
    args += [kc, vc, sink_cols, u, vv, ws, gb, wout, mod, h]
    return pl.pallas_call(
        functools.partial(_even_mixer_kernel, local=local, seq=s),
        grid=(b, s // tq),
        in_specs=in_specs,
        out_specs=tok(d),
        out_shape=jax.ShapeDtypeStruct((b, s, d), F32),
        scratch_shapes=[pltpu.VMEM((tq, d), BF16)],
        compiler_params=_params("parallel", "parallel"),
        name="even_mixer" if local else "even_mixer_ctx",
    )(*args)


def _odd_inproj_kernel(h_ref, mod_ref, g_ref, win_ref, qg_ref, kg_ref, ones_ref, *out_refs, kv_only):
    z = _norm_mod(h_ref[...], g_ref[...], mod_ref[3:4, :], mod_ref[4:5, :]).astype(BF16)
    ones = ones_ref[...]
    w = MIX_HALF
    if kv_only:
        k_ref, v_ref = out_refs
    else:
        xp_ref, q_ref, k_ref, v_ref = out_refs
        xp_ref[...] = _dot(z, win_ref[:, 0:w])
        q = _head_rms(_dot(z, win_ref[:, w:2 * w]), qg_ref[...], ones)
        q_ref[...] = (q * HEAD_DIM ** -0.5).astype(BF16)
    k_ref[...] = _head_rms(_dot(z, win_ref[:, 2 * w:3 * w]), kg_ref[...], ones).astype(BF16)
    v_ref[...] = _dot(z, win_ref[:, 3 * w:4 * w]).astype(BF16)


def _odd_inproj(h, mod, mod_row_of_batch, g, win, qg, kg, ones, kv_only):
    b, s, d = h.shape
    tm = min(TOK_TILE, s)
    tok = lambda w: pl.BlockSpec((None, tm, w), lambda bi, i: (bi, i, 0))
    const = lambda shape: _resident(shape, lambda bi, i: (0,) * len(shape))
    bf = jax.ShapeDtypeStruct((b, s, MIX_HALF), BF16)
    if kv_only:
        out_specs, out_shape = [tok(MIX_HALF)] * 2, [bf, bf]
    else:
        out_specs = [tok(MIX_HALF)] * 4
        out_shape = [jax.ShapeDtypeStruct((b, s, MIX_HALF), F32), bf, bf, bf]
    return pl.pallas_call(
        functools.partial(_odd_inproj_kernel, kv_only=kv_only),
        grid=(b, s // tm),
        in_specs=[
            tok(d),
            pl.BlockSpec((None, N_MOD, d), lambda bi, i: (mod_row_of_batch(bi), 0, 0)),
            const((1, d)), const((d, ODD_IN)), const((1, MIX_HALF)), const((1, MIX_HALF)),
            const((MXU_DIM, MXU_DIM)),
        ],
        out_specs=out_specs,
        out_shape=out_shape,
        compiler_params=_params("parallel", "parallel"),
        name="odd_inproj_kv" if kv_only else "odd_inproj",
    )(h, mod, g.reshape(1, d), win, qg, kg, ones)


def _na_tables(rows):
    nrb = rows // NA_QROWS
    nr, nc = 2 * NA_ROWS - 1, 2 * NA_COLS - 1
    rsel = np.zeros((3, NA_QROWS, NA_KROWS, nr), np.float32)
    rvalid = np.zeros((3, NA_QROWS, NA_KROWS), bool)
    for rv, a in enumerate((0, nrb // 2, nrb - 1)):
        ks = int(np.clip(NA_QROWS * a - NA_ROWS // 2, 0, rows - NA_KROWS))
        r = NA_QROWS * a + np.arange(NA_QROWS)[:, None]
        kr = ks + np.arange(NA_KROWS)[None, :]
        r0 = np.clip(r - NA_ROWS // 2, 0, rows - NA_ROWS)
        rvalid[rv] = (kr >= r0) & (kr < r0 + NA_ROWS)
        ri = np.clip(kr - r + NA_ROWS - 1, 0, nr - 1)
        rsel[rv] = np.arange(nr) == ri[..., None]
    csel = np.zeros((3, NA_COLS, NA_KCOLS, nc), np.float32)
    cvalid = np.zeros((3, NA_COLS, NA_KCOLS), bool)
    for cv, m in enumerate((0, 1, 3)):
        qc = NA_COLS * m + np.arange(NA_COLS)[:, None]
        kcol = NA_KSTART[m] + np.arange(NA_KCOLS)[None, :]
        qstart = np.clip(qc - NA_COLS // 2, 0, GRID_W - NA_COLS)
        cvalid[cv] = (kcol >= qstart) & (kcol < qstart + NA_COLS)
        ci = np.clip(kcol - qc, 1 - NA_COLS, NA_COLS - 1) + NA_COLS - 1
        csel[cv] = np.arange(nc) == ci[..., None]
    valid = rvalid[:, None, None, :, None, :, None] & cvalid[None, :, None, None, :, None, :]
    return rsel, csel, valid


def _na_bias_table(rpb, rows):
    rsel, csel, valid = _na_tables(rows)
    tbl = jnp.einsum("vrka,hab,cqjb->vchrqkj", rsel, rpb, csel, precision=lax.Precision.HIGHEST)
    tbl = jnp.where(valid, tbl, NEG)
    return tbl.reshape(3, 3, rpb.shape[0], NA_QROWS * NA_COLS, NA_KROWS * NA_KCOLS)


def _odd_mixer_kernel(xp_ref, xprev_ref, xnext_ref, q_ref, k_ref, v_ref, kc_ref, vc_ref, bias_ref,
                      wpool_ref, cs_ref, wout_ref, mod_ref, h_ref, o_ref, mix_ref, xw_ref, *, seq):
    a = pl.program_id(1)
    last = pl.num_programs(1) - 1
    tq = q_ref.shape[0]
    halo = xprev_ref.shape[0]
    lane = lax.broadcasted_iota(jnp.int32, (1, LANES), 1)
    low = lane < HEAD_DIM

    xw_ref[0:halo, :] = jnp.where(a == 0, 0.0, xprev_ref[...])
    xw_ref[halo:halo + tq, :] = xp_ref[...]
    xw_ref[halo + tq:halo + tq + halo, :] = jnp.where(a == last, 0.0, xnext_ref[...])
    t = a * tq + lax.broadcasted_iota(jnp.int32, (tq, 1), 0)
    for gi, w in enumerate(POOL_WINDOWS):
        cols = slice(gi * C_GROUP_W, (gi + 1) * C_GROUP_W)
        acc = None
        for dlt in range(-(w // 2), w - w // 2):
            term = xw_ref[halo + dlt:halo + dlt + tq, cols]
            acc = term if acc is None else acc + term
        cnt = (jnp.minimum(t + (w - w // 2), seq) - jnp.maximum(t - w // 2, 0)).astype(F32)
        pooled = acc / cnt - xp_ref[:, cols]
        y = _dot(pooled.astype(BF16), wpool_ref[gi]) * cs_ref[:, cols]
        mix_ref[:, cols] = y.astype(BF16)

    rows = seq // GRID_W
    ks = jnp.clip(NA_QROWS * a - NA_ROWS // 2, 0, rows - NA_KROWS)
    tok0 = pl.multiple_of(ks * GRID_W, GRID_W)
    nkw = NA_KROWS * GRID_W
    for p in range(MIX_HALF // LANES):
        cols = slice(p * LANES, (p + 1) * LANES)
        kwin = k_ref[pl.ds(tok0, nkw), cols].astype(F32).reshape(NA_KROWS, GRID_W, LANES)
        vwin = v_ref[pl.ds(tok0, nkw), cols].astype(F32).reshape(NA_KROWS, GRID_W, LANES)
        kcp = kc_ref[:, cols]
        vcp = vc_ref[:, cols]
        for m in range(GRID_W // NA_COLS):
            kst = NA_KSTART[m]
            nloc = NA_KROWS * NA_KCOLS
            kl = kwin[:, kst:kst + NA_KCOLS, :].reshape(nloc, LANES).astype(BF16)
            vl = vwin[:, kst:kst + NA_KCOLS, :].reshape(nloc, LANES).astype(BF16)
            kcat = jnp.concatenate([kl, kcp], axis=0)
            vcat = jnp.concatenate([vl, vcp], axis=0)
            qm = jnp.concatenate(
                [q_ref[r * GRID_W + m * NA_COLS:r * GRID_W + (m + 1) * NA_COLS, cols]
                 for r in range(NA_QROWS)], axis=0)
            outs = []
            for hh in range(2):
                keep = low if hh == 0 else jnp.logical_not(low)
                s = _dot_nt(jnp.where(keep, qm, jnp.zeros((), BF16)), kcat)
                s = jnp.concatenate([s[:, :nloc] + bias_ref[NA_COLVAR[m], 2 * p + hh], s[:, nloc:]], axis=1)
                outs.append(_softmax_pv(s, None, vcat))
            o = jnp.where(low, outs[0], outs[1]).astype(BF16)
            for r in range(NA_QROWS):
                mix_ref[r * GRID_W + m * NA_COLS:r * GRID_W + (m + 1) * NA_COLS,
                        MIX_HALF + p * LANES:MIX_HALF + (p + 1) * LANES] = o[r * NA_COLS:(r + 1) * NA_COLS]
    y = _dot(mix_ref[...], wout_ref[...])
    o_ref[...] = h_ref[...] + mod_ref[5:6, :] * y


def _odd_mixer(xp, q, k, v, kc, vc, bias_tbl, wpool, cscale, wout, mod, h):
    b, s, d = h.shape
    lc = kc.shape[1]
    tq = NA_QROWS * GRID_W
    nrb = s // tq
    halo = 8
    hb = tq // halo
    tok = lambda w: pl.BlockSpec((None, tq, w), lambda bi, i: (bi, i, 0))
    per_batch = lambda rows, w: _resident((None, rows, w), lambda bi, i: (bi, 0, 0))
    const = lambda shape: _resident(shape, lambda bi, i: (0,) * len(shape))

    def row_variant(i):
        return jnp.where(i == 0, 0, jnp.where(i == nrb - 1, 2, 1))

    return pl.pallas_call(
        functools.partial(_odd_mixer_kernel, seq=s),
        grid=(b, nrb),
        in_specs=[
            tok(MIX_HALF),
            pl.BlockSpec((None, halo, MIX_HALF), lambda bi, i: (bi, jnp.maximum(i * hb - 1, 0), 0)),
            pl.BlockSpec((None, halo, MIX_HALF), lambda bi, i: (bi, jnp.minimum((i + 1) * hb, s // halo - 1), 0)),
            tok(MIX_HALF),
            per_batch(s, MIX_HALF), per_batch(s, MIX_HALF),
            per_batch(lc, MIX_HALF), per_batch(lc, MIX_HALF),
            pl.BlockSpec((None,) + bias_tbl.shape[1:], lambda bi, i: (row_variant(i), 0, 0, 0, 0)),
            const(wpool.shape), const(cscale.shape), const((d, d)),
            pl.BlockSpec((None, N_MOD, d), lambda bi, i: (bi, 0, 0)),
            tok(d),
        ],
        out_specs=tok(d),
        out_shape=jax.ShapeDtypeStruct((b, s, d), F32),
        scratch_shapes=[pltpu.VMEM((tq, d), BF16), pltpu.VMEM((tq + 2 * halo, MIX_HALF), F32)],
        compiler_params=_params("parallel", "arbitrary"),
        name="odd_mixer",
    )(xp, xp, xp, q, k, v, kc, vc, bias_tbl, wpool, cscale, wout, mod, h)


def _rope_tables(s):
    t = jnp.arange(s)
    row = (t // GRID_W).astype(F32)
    col = (t % GRID_W).astype(F32)
    m = HEAD_DIM // 4
    inv = 1.0 / (ROPE_THETA ** (jnp.arange(m, dtype=F32) / m))
    ang_r = row[:, None] * inv[None, :]
    ang_c = col[:, None] * inv[None, :]
    cos_h = jnp.concatenate([jnp.cos(ang_r)] * 2 + [jnp.cos(ang_c)] * 2, axis=-1)
    sin_h = jnp.concatenate([-jnp.sin(ang_r), jnp.sin(ang_r), -jnp.sin(ang_c), jnp.sin(ang_c)], axis=-1)
    reps = LANES // HEAD_DIM
    return jnp.tile(cos_h, (1, reps)), jnp.tile(sin_h, (1, reps))


def _block_diag_ones():
    idx = np.arange(MXU_DIM) // HEAD_DIM
    return jnp.asarray(idx[:, None] == idx[None, :], dtype=BF16)


def kernel(x, c, ctx, c_ctx, ada_w, ada_b, norm_g, ffn_w_gu, ffn_w_down, ev_w_in, ev_w_out,
           a_q_gain, a_k_gain, a_sink, b_v_gain, b_ws, b_bias, od_w_in, od_w_out, c_w_pool,
           c_scale, d_q_gain, d_k_gain, d_rpb):
    b, s, d = x.shape
    lc = ctx.shape[1]
    depth = ada_w.shape[0]
    assert d == D_MODEL and depth == 2 and s % TOK_TILE == 0 and s // GRID_W >= NA_KROWS
    assert lc % B_CHUNK == 0 and b <= 8

    mod_rows = 16
    cvec = jnp.zeros((mod_rows, d), F32).at[:b].set(c).at[b].set(c_ctx)
    mod = _modulation(cvec, ada_w, ada_b)
    lat_row = lambda bi: bi
    ctx_row = lambda bi: b

    ones = _block_diag_ones()
    cos, sin = _rope_tables(s)
    cos_id = jnp.ones((lc, LANES), F32)
    sin_id = jnp.zeros((lc, LANES), F32)
    tile_heads = lambda gain, n: jnp.tile(gain, n).reshape(1, n * HEAD_DIM)

    h, hc = x, ctx
    li, e = 0, 0
    wgu = ffn_w_gu[li].astype(BF16)
    wd = ffn_w_down[li].astype(BF16)
    h = _half_ffn(h, mod[li], lat_row, norm_g[li, 0], wgu[0], wd[0], 0)
    hc = _half_ffn(hc, mod[li], ctx_row, norm_g[li, 0], wgu[0], wd[0], 0)
    hsl = lambda hd: slice(hd * HEAD_DIM, (hd + 1) * HEAD_DIM)
    win = jnp.concatenate([ev_w_in[e][:, hsl(hd)] for hd in A_HEAD_PERM] + [ev_w_in[e][:, MIX_HALF:]],
                          axis=1).astype(BF16)
    wout = jnp.concatenate([ev_w_out[e][hsl(hd)] for hd in A_HEAD_PERM] + [ev_w_out[e][MIX_HALF:]],
                           axis=0).astype(BF16)
    qg = tile_heads(a_q_gain[e], N_HEADS)
    kg = tile_heads(a_k_gain[e], A_KV_HEADS)
    vg = b_v_gain[e].reshape(1, MIX_HALF)
    q, k, v, u, vv = _even_inproj(h, mod[li], lat_row, norm_g[li, 1], win, qg, kg, vg, cos, sin, ones)
    qc, kc, vc, uc, vvc = _even_inproj(hc, mod[li], ctx_row, norm_g[li, 1], win, qg, kg, vg,
                                       cos_id, sin_id, ones)
    sink_cols = jnp.repeat(a_sink[e].reshape(A_KV_HEADS, N_HEADS // A_KV_HEADS), A_BLOCK, axis=1)[..., None]
    ws = b_ws[e].astype(BF16)
    gb = jnp.repeat(b_bias[e].T, HEAD_DIM, axis=1)
    h = _even_mixer(q, k, v, kc, vc, sink_cols, u, vv, ws, gb, wout, mod[li], lat_row, h, True)
    hc = _even_mixer(qc, None, None, kc, vc, sink_cols, uc, vvc, ws, gb, wout, mod[li], ctx_row, hc, False)
    h = _half_ffn(h, mod[li], lat_row, norm_g[li, 2], wgu[1], wd[1], 6)
    hc = _half_ffn(hc, mod[li], ctx_row, norm_g[li, 2], wgu[1], wd[1], 6)

    li, o = 1, 0
    wgu = ffn_w_gu[li].astype(BF16)
    wd = ffn_w_down[li].astype(BF16)
    h = _half_ffn(h, mod[li], lat_row, norm_g[li, 0], wgu[0], wd[0], 0)
    hc = _half_ffn(hc, mod[li], ctx_row, norm_g[li, 0], wgu[0], wd[0], 0)
    win = od_w_in[o].astype(BF16)
    wout = od_w_out[o].astype(BF16)
    qg = tile_heads(d_q_gain[o], N_HEADS)
    kg = tile_heads(d_k_gain[o], N_HEADS)
    xp, q, k, v = _odd_inproj(h, mod[li], lat_row, norm_g[li, 1], win, qg, kg, ones, False)
    kc, vc = _odd_inproj(hc, mod[li], ctx_row, norm_g[li, 1], win, qg, kg, ones, True)
    bias_tbl = _na_bias_table(d_rpb[o], s // GRID_W)
    h = _odd_mixer(xp, q, k, v, kc, vc, bias_tbl, c_w_pool[o].astype(BF16), c_scale[o].reshape(1, MIX_HALF),
                   wout, mod[li], h)
    h = _half_ffn(h, mod[li], lat_row, norm_g[li, 2], wgu[1], wd[1], 6)
    return h
```

```python
import functools

import numpy as np
import jax
import jax.numpy as jnp
from jax import lax
from jax.experimental import pallas as pl
from jax.experimental.pallas import tpu as pltpu

F32 = jnp.float32
BF16 = jnp.bfloat16

D_MODEL = 1024
GRID_W = 64
HEAD_DIM = 64
MIX_HALF = D_MODEL // 2
N_MOD = 9
D_FF = ((8 * D_MODEL // 3 + 127) // 128) * 128
EPS = 1e-6
ROPE_THETA = 10000.0
NEG = -1e30
N_HEADS = MIX_HALF // HEAD_DIM
A_KV_HEADS = N_HEADS // 4
A_KV = A_KV_HEADS * HEAD_DIM
WINDOW = 128
A_BLOCK = 128
B_CHUNK = 128
EVEN_IN = MIX_HALF + 2 * A_KV + 2 * MIX_HALF
POOL_WINDOWS = (2, 4, 8, 16)
C_GROUP_W = MIX_HALF // len(POOL_WINDOWS)
NA_ROWS = 8
NA_COLS = 16
ODD_IN = 4 * MIX_HALF

LANES = 128
MXU_DIM = 256
VMEM_LIMIT = 56 * 1024 * 1024

LOG2E = 1.4426950408889634
Q_SCALE = HEAD_DIM ** -0.5 * LOG2E
QK_AHEAD = 2

TOK_TILE = 512
EVEN_TQ = 512
NA_QROWS = 8
NA_KROWS = 16
NA_KCOLS = 2 * NA_COLS
NA_KSTART = tuple(int(v) for v in np.clip(np.arange(GRID_W // NA_COLS) * NA_COLS - NA_COLS // 2,
                                         0, GRID_W - NA_KCOLS))
NA_COLVAR = (0, 1, 1, 2)
FFN_CHUNKS = (512, 512, 512, 512, 512, 256)
assert sum(FFN_CHUNKS) == D_FF

A_HEAD_PERM = (0, 4, 1, 5, 2, 6, 3, 7)


def _params(*sem):
    return pltpu.CompilerParams(dimension_semantics=sem, vmem_limit_bytes=VMEM_LIMIT)


def _resident(shape, index_map):
    return pl.BlockSpec(shape, index_map, pipeline_mode=pl.Buffered(1))


def _norm_mod(h, g, shift, scale):
    y = h * lax.rsqrt(jnp.mean(h * h, axis=-1, keepdims=True) + EPS)
    return (y * g) * (1.0 + scale) + shift


def _dot(a, b):
    return jnp.dot(a, b, preferred_element_type=F32)


def _dot_nt(a, b):
    return lax.dot_general(a, b, (((1,), (1,)), ((), ())), preferred_element_type=F32)


def _head_rms(x, gain, ones_bd):
    w = x.shape[1]
    step = min(w, MXU_DIM)
    ones = ones_bd[:step, :step]
    parts = []
    for c in range(0, w, step):
        xs = x[:, c:c + step]
        x2 = xs * xs
        hi = x2.astype(BF16)
        lo = (x2 - hi.astype(F32)).astype(BF16)
        ss = _dot(hi, ones) + _dot(lo, ones)
        parts.append(xs * lax.rsqrt(ss / HEAD_DIM + EPS))
    y = parts[0] if len(parts) == 1 else jnp.concatenate(parts, axis=1)
    return y * gain


def _rope(x, cos, sin_signed):
    lane = lax.broadcasted_iota(jnp.int32, (1, LANES), 1)
    first = (lane & 16) == 0
    parts = []
    for c in range(0, x.shape[1], LANES):
        xb = x[:, c:c + LANES]
        partner = jnp.where(first, pltpu.roll(xb, LANES - 16, 1), pltpu.roll(xb, 16, 1))
        parts.append(xb * cos + partner * sin_signed)
    return parts[0] if len(parts) == 1 else jnp.concatenate(parts, axis=1)


def _softmax_pv(s, sink, vcat):
    m = jnp.max(s, axis=-1, keepdims=True)
    if sink is not None:
        m = jnp.maximum(m, sink)
    p = jnp.exp2(s - m)
    den = jnp.sum(p, axis=-1, keepdims=True)
    if sink is not None:
        den = den + jnp.exp2(sink - m)
    return _dot(p.astype(BF16), vcat) / den


def _pipelined(n, ahead, stage_a, stage_b):
    pending = [stage_a(i) for i in range(min(ahead, n))]
    for i in range(n):
        if i + ahead < n:
            pending.append(stage_a(i + ahead))
        stage_b(i, pending.pop(0))


def _mod_kernel(c_ref, w_ref, b_ref, o_ref):
    sc = jax.nn.silu(c_ref[...])
    o_ref[...] = _dot(sc, w_ref[...]) + b_ref[...]


def _modulation(cvec, ada_w, ada_b):
    depth, d, n = ada_w.shape
    r = cvec.shape[0]
    tn = 1024
    out = pl.pallas_call(
        _mod_kernel,
        grid=(depth, n // tn),
        in_specs=[
            pl.BlockSpec((r, d), lambda l, j: (0, 0)),
            pl.BlockSpec((None, d, tn), lambda l, j: (l, 0, j)),
            pl.BlockSpec((None, 1, tn), lambda l, j: (l, 0, j)),
        ],
        out_specs=pl.BlockSpec((None, r, tn), lambda l, j: (l, 0, j)),
        out_shape=jax.ShapeDtypeStruct((depth, r, n), F32),
        compiler_params=_params("parallel", "parallel"),
        name="adaln_mod",
    )(cvec, ada_w, ada_b.reshape(depth, 1, n))
    return out.reshape(depth, r, N_MOD, d)


def _ffn_kernel(h_ref, mod_ref, g_ref, wgu_ref, wd_ref, o_ref, *, mrow):
    h = h_ref[...]
    z = _norm_mod(h, g_ref[...], mod_ref[mrow:mrow + 1, :], mod_ref[mrow + 1:mrow + 2, :]).astype(BF16)
    acc = None
    off = 0
    for ch in FFN_CHUNKS:
        gt = _dot(z, wgu_ref[:, off:off + ch])
        up = _dot(z, wgu_ref[:, D_FF + off:D_FF + off + ch])
        a = (jax.nn.silu(gt) * up).astype(BF16)
        part = _dot(a, wd_ref[off:off + ch, :])
        acc = part if acc is None else acc + part
        off += ch
    o_ref[...] = h + (0.5 * mod_ref[mrow + 2:mrow + 3, :]) * acc


def _half_ffn(h, mod, mod_row_of_batch, g, wgu, wd, mrow):
    b, s, d = h.shape
    tm = min(TOK_TILE, s)
    return pl.pallas_call(
        functools.partial(_ffn_kernel, mrow=mrow),
        grid=(b, s // tm),
        in_specs=[
            pl.BlockSpec((None, tm, d), lambda bi, i: (bi, i, 0)),
            pl.BlockSpec((None, N_MOD, d), lambda bi, i: (mod_row_of_batch(bi), 0, 0)),
            _resident((1, d), lambda bi, i: (0, 0)),
            _resident((d, 2 * D_FF), lambda bi, i: (0, 0)),
            _resident((D_FF, d), lambda bi, i: (0, 0)),
        ],
        out_specs=pl.BlockSpec((None, tm, d), lambda bi, i: (bi, i, 0)),
        out_shape=jax.ShapeDtypeStruct((b, s, d), F32),
        compiler_params=_params("parallel", "parallel"),
        name="half_ffn",
    )(h, mod, g.reshape(1, d), wgu, wd)


def _even_inproj_kernel(h_ref, mod_ref, g_ref, win_ref, qg_ref, kg_ref, vg_ref, cos_ref, sin_ref,
                        ones_ref, q_ref, k_ref, v_ref, u_ref, vv_ref):
    z = _norm_mod(h_ref[...], g_ref[...], mod_ref[3:4, :], mod_ref[4:5, :]).astype(BF16)
    ones = ones_ref[...]
    cos = cos_ref[...]
    sin = sin_ref[...]
    cq, ck, cv, cu, cb = 0, MIX_HALF, MIX_HALF + A_KV, MIX_HALF + 2 * A_KV, 2 * MIX_HALF + 2 * A_KV
    q = _dot(z, win_ref[:, cq:cq + MIX_HALF])
    k = _dot(z, win_ref[:, ck:ck + A_KV])
    q = _rope(_head_rms(q, qg_ref[...], ones), cos, sin)
    q_ref[...] = (q * Q_SCALE).astype(BF16)
    bv = jax.nn.gelu(_dot(z, win_ref[:, cb:cb + MIX_HALF]))
    k_ref[...] = _rope(_head_rms(k, kg_ref[...], ones), cos, sin).astype(BF16)
    v_ref[...] = _dot(z, win_ref[:, cv:cv + A_KV]).astype(BF16)
    u = _dot(z, win_ref[:, cu:cu + MIX_HALF])
    vv_ref[...] = _head_rms(bv, vg_ref[...], ones).astype(BF16)
    u_ref[...] = jax.nn.gelu(u)


def _even_inproj(h, mod, mod_row_of_batch, g, win, qg, kg, vg, cos, sin, ones):
    b, s, d = h.shape
    tm = min(TOK_TILE, s)
    tok = lambda w: pl.BlockSpec((None, tm, w), lambda bi, i: (bi, i, 0))
    const = lambda shape: _resident(shape, lambda bi, i: (0,) * len(shape))
    return pl.pallas_call(
        _even_inproj_kernel,
        grid=(b, s // tm),
        in_specs=[
            tok(d),
            pl.BlockSpec((None, N_MOD, d), lambda bi, i: (mod_row_of_batch(bi), 0, 0)),
            const((1, d)), const((d, EVEN_IN)), const((1, MIX_HALF)), const((1, A_KV)),
            const((1, MIX_HALF)),
            pl.BlockSpec((tm, LANES), lambda bi, i: (i, 0)),
            pl.BlockSpec((tm, LANES), lambda bi, i: (i, 0)),
            const((MXU_DIM, MXU_DIM)),
        ],
        out_specs=[tok(MIX_HALF), tok(A_KV), tok(A_KV), tok(MIX_HALF), tok(MIX_HALF)],
        out_shape=[
            jax.ShapeDtypeStruct((b, s, MIX_HALF), BF16),
            jax.ShapeDtypeStruct((b, s, A_KV), BF16),
            jax.ShapeDtypeStruct((b, s, A_KV), BF16),
            jax.ShapeDtypeStruct((b, s, MIX_HALF), F32),
            jax.ShapeDtypeStruct((b, s, MIX_HALF), BF16),
        ],
        compiler_params=_params("parallel", "parallel"),
        name="even_inproj",
    )(h, mod, g.reshape(1, d), win, qg, kg, vg, cos, sin, ones)


def _even_mixer_kernel(*refs, local, seq):
    if local:
        (q_ref, k_ref, v_ref, kc_ref, vc_ref, sink_ref, u_ref, vv_ref, ws_ref, gb_ref, wout_ref,
         mod_ref, h_ref, o_ref, mix_ref) = refs
    else:
        (q_ref, kc_ref, vc_ref, sink_ref, u_ref, vv_ref, ws_ref, gb_ref, wout_ref,
         mod_ref, h_ref, o_ref, mix_ref) = refs
    tq = q_ref.shape[0]
    nblk = tq // A_BLOCK
    nloc = 3 * A_BLOCK
    lane = lax.broadcasted_iota(jnp.int32, (1, LANES), 1)
    low = lane < HEAD_DIM
    kc = kc_ref[...]
    vc = vc_ref[...]
    groups = N_HEADS // A_KV_HEADS
    nstack = A_KV_HEADS * groups

    def scores(blk):
        r0 = blk * A_BLOCK
        vvb = vv_ref[r0:r0 + B_CHUNK, :]
        for p in range(MIX_HALF // LANES):
            cols = slice(p * LANES, (p + 1) * LANES)
            vsl = vvb[:, cols]
            mixed = jnp.where(low, _dot(ws_ref[2 * p], vsl), _dot(ws_ref[2 * p + 1], vsl)) + gb_ref[:, cols]
            mix_ref[r0:r0 + B_CHUNK, MIX_HALF + p * LANES:MIX_HALF + (p + 1) * LANES] = (
                u_ref[r0:r0 + B_CHUNK, cols] * mixed).astype(BF16)
        qb = q_ref[r0:r0 + A_BLOCK, :]
        q8 = jnp.concatenate(
            [jnp.where(low if j == 0 else jnp.logical_not(low), qb[:, p * LANES:(p + 1) * LANES],
                       jnp.zeros((), BF16))
             for j in range(A_KV_HEADS) for p in range(groups)], axis=0)
        if not local:
            return _dot_nt(q8, kc), vc
        n = pl.program_id(1) * nblk + blk
        start = pl.multiple_of(jnp.clip((n - 1) * A_BLOCK, 0, seq - nloc), A_BLOCK)
        kcat = jnp.concatenate([k_ref[pl.ds(start, nloc), :], kc], axis=0)
        vcat = jnp.concatenate([v_ref[pl.ds(start, nloc), :], vc], axis=0)
        qpos = n * A_BLOCK + lax.broadcasted_iota(jnp.int32, (A_BLOCK, 1), 0)
        kpos = start + lax.broadcasted_iota(jnp.int32, (1, nloc), 1)
        mask = jnp.where(jnp.abs(qpos - kpos) <= WINDOW, 0.0, NEG)
        s = _dot_nt(q8, kcat)
        s_loc = (s[:, :nloc].reshape(nstack, A_BLOCK, nloc) + mask[None]).reshape(nstack * A_BLOCK, nloc)
        return jnp.concatenate([s_loc, s[:, nloc:]], axis=1), vcat

    def values(blk, sv):
        r0 = blk * A_BLOCK
        o = _softmax_pv(sv[0], sink_ref[...], sv[1])
        for p in range(groups):
            lo_rows = o[p * A_BLOCK:(p + 1) * A_BLOCK]
            hi_rows = o[(groups + p) * A_BLOCK:(groups + p + 1) * A_BLOCK]
            mix_ref[r0:r0 + A_BLOCK, p * LANES:(p + 1) * LANES] = jnp.where(low, lo_rows, hi_rows).astype(BF16)

    _pipelined(nblk, QK_AHEAD, scores, values)
    y = _dot(mix_ref[...], wout_ref[...])
    o_ref[...] = h_ref[...] + mod_ref[5:6, :] * y


def _even_mixer(q, k, v, kc, vc, sink_cols, u, vv, ws, gb, wout, mod, mod_row_of_batch, h, local):
    b, s, d = h.shape
    lc = kc.shape[1]
    tq = min(EVEN_TQ, s)
    tok = lambda w: pl.BlockSpec((None, tq, w), lambda bi, i: (bi, i, 0))
    per_batch = lambda rows, w: pl.BlockSpec((None, rows, w), lambda bi, i: (bi, 0, 0))
    const = lambda shape: _resident(shape, lambda bi, i: (0,) * len(shape))
    in_specs = [tok(MIX_HALF)]
    args = [q]
    if local:
        in_specs += [per_batch(s, A_KV), per_batch(s, A_KV)]
        args += [k, v]
    in_specs += [
        per_batch(lc, A_KV), per_batch(lc, A_KV),
        const(sink_cols.shape),
        tok(MIX_HALF), tok(MIX_HALF),
        const(ws.shape), const(gb.shape), const((d, d)),
        pl.BlockSpec((None, N_MOD, d), lambda bi, i: (mod_row_of_batch(bi), 0, 0)),
        tok(d),
    ]
    args += [kc, vc, sink_cols, u, vv, ws, gb, wout, mod, h]
    return pl.pallas_call(
        functools.partial(_even_mixer_kernel, local=local, seq=s),
        grid=(b, s // tq),
        in_specs=in_specs,
        out_specs=tok(d),
        out_shape=jax.ShapeDtypeStruct((b, s, d), F32),
        scratch_shapes=[pltpu.VMEM((tq, d), BF16)],
        compiler_params=_params("parallel", "parallel"),
        name="even_mixer" if local else "even_mixer_ctx",
    )(*args)


def _odd_inproj_kernel(h_ref, mod_ref, g_ref, win_ref, qg_ref, kg_ref, ones_ref, *out_refs, kv_only):
    z = _norm_mod(h_ref[...], g_ref[...], mod_ref[3:4, :], mod_ref[4:5, :]).astype(BF16)
    ones = ones_ref[...]
    w = MIX_HALF
    if kv_only:
        k_ref, v_ref = out_refs
        k = _dot(z, win_ref[:, 2 * w:3 * w])
    else:
        xp_ref, q_ref, k_ref, v_ref = out_refs
        q = _dot(z, win_ref[:, w:2 * w])
        k = _dot(z, win_ref[:, 2 * w:3 * w])
        q = _head_rms(q, qg_ref[...], ones)
        q_ref[...] = (q * Q_SCALE).astype(BF16)
        xp_ref[...] = _dot(z, win_ref[:, 0:w])
    v = _dot(z, win_ref[:, 3 * w:4 * w])
    k_ref[...] = _head_rms(k, kg_ref[...], ones).astype(BF16)
    v_ref[...] = v.astype(BF16)


def _odd_inproj(h, mod, mod_row_of_batch, g, win, qg, kg, ones, kv_only):
    b, s, d = h.shape
    tm = min(TOK_TILE, s)
    tok = lambda w: pl.BlockSpec((None, tm, w), lambda bi, i: (bi, i, 0))
    const = lambda shape: _resident(shape, lambda bi, i: (0,) * len(shape))
    bf = jax.ShapeDtypeStruct((b, s, MIX_HALF), BF16)
    if kv_only:
        out_specs, out_shape = [tok(MIX_HALF)] * 2, [bf, bf]
    else:
        out_specs = [tok(MIX_HALF)] * 4
        out_shape = [jax.ShapeDtypeStruct((b, s, MIX_HALF), F32), bf, bf, bf]
    return pl.pallas_call(
        functools.partial(_odd_inproj_kernel, kv_only=kv_only),
        grid=(b, s // tm),
        in_specs=[
            tok(d),
            pl.BlockSpec((None, N_MOD, d), lambda bi, i: (mod_row_of_batch(bi), 0, 0)),
            const((1, d)), const((d, ODD_IN)), const((1, MIX_HALF)), const((1, MIX_HALF)),
            const((MXU_DIM, MXU_DIM)),
        ],
        out_specs=out_specs,
        out_shape=out_shape,
        compiler_params=_params("parallel", "parallel"),
        name="odd_inproj_kv" if kv_only else "odd_inproj",
    )(h, mod, g.reshape(1, d), win, qg, kg, ones)


def _na_tables(rows):
    nrb = rows // NA_QROWS
    nr, nc = 2 * NA_ROWS - 1, 2 * NA_COLS - 1
    rsel = np.zeros((3, NA_QROWS, NA_KROWS, nr), np.float32)
    rvalid = np.zeros((3, NA_QROWS, NA_KROWS), bool)
    for rv, a in enumerate((0, nrb // 2, nrb - 1)):
        ks = int(np.clip(NA_QROWS * a - NA_ROWS // 2, 0, rows - NA_KROWS))
        r = NA_QROWS * a + np.arange(NA_QROWS)[:, None]
        kr = ks + np.arange(NA_KROWS)[None, :]
        r0 = np.clip(r - NA_ROWS // 2, 0, rows - NA_ROWS)
        rvalid[rv] = (kr >= r0) & (kr < r0 + NA_ROWS)
        ri = np.clip(kr - r + NA_ROWS - 1, 0, nr - 1)
        rsel[rv] = np.arange(nr) == ri[..., None]
    csel = np.zeros((3, NA_COLS, NA_KCOLS, nc), np.float32)
    cvalid = np.zeros((3, NA_COLS, NA_KCOLS), bool)
    for cv, m in enumerate((0, 1, 3)):
        qc = NA_COLS * m + np.arange(NA_COLS)[:, None]
        kcol = NA_KSTART[m] + np.arange(NA_KCOLS)[None, :]
        qstart = np.clip(qc - NA_COLS // 2, 0, GRID_W - NA_COLS)
        cvalid[cv] = (kcol >= qstart) & (kcol < qstart + NA_COLS)
        ci = np.clip(kcol - qc, 1 - NA_COLS, NA_COLS - 1) + NA_COLS - 1
        csel[cv] = np.arange(nc) == ci[..., None]
    valid = rvalid[:, None, None, :, None, :, None] & cvalid[None, :, None, None, :, None, :]
    return rsel, csel, valid


def _na_bias_table(rpb, rows):
    rsel, csel, valid = _na_tables(rows)
    tbl = jnp.einsum("vrka,hab,cqjb->vchrqkj", rsel, rpb, csel, precision=lax.Precision.HIGHEST)
    tbl = jnp.where(valid, tbl * LOG2E, NEG)
    return tbl.reshape(3, 3, rpb.shape[0], NA_QROWS * NA_COLS, NA_KROWS * NA_KCOLS)


def _odd_mixer_kernel(xp_ref, xprev_ref, xnext_ref, q_ref, k_ref, v_ref, kc_ref, vc_ref, bias_ref,
                      wpool_ref, cs_ref, wout_ref, mod_ref, h_ref, o_ref, mix_ref, xw_ref, kw_ref, vw_ref,
                      *, seq):
    a = pl.program_id(1)
    last = pl.num_programs(1) - 1
    tq = q_ref.shape[0]
    halo = xprev_ref.shape[0]
    lane = lax.broadcasted_iota(jnp.int32, (1, LANES), 1)
    low = lane < HEAD_DIM

    xw_ref[0:halo, :] = jnp.where(a == 0, 0.0, xprev_ref[...])
    xw_ref[halo:halo + tq, :] = xp_ref[...]
    xw_ref[halo + tq:halo + tq + halo, :] = jnp.where(a == last, 0.0, xnext_ref[...])
    t = a * tq + lax.broadcasted_iota(jnp.int32, (tq, 1), 0)
    for gi, w in enumerate(POOL_WINDOWS):
        cols = slice(gi * C_GROUP_W, (gi + 1) * C_GROUP_W)
        acc = None
        for dlt in range(-(w // 2), w - w // 2):
            term = xw_ref[halo + dlt:halo + dlt + tq, cols]
            acc = term if acc is None else acc + term
        cnt = (jnp.minimum(t + (w - w // 2), seq) - jnp.maximum(t - w // 2, 0)).astype(F32)
        pooled = acc / cnt - xp_ref[:, cols]
        y = _dot(pooled.astype(BF16), wpool_ref[gi]) * cs_ref[:, cols]
        mix_ref[:, cols] = y.astype(BF16)

    rows = seq // GRID_W
    ks = jnp.clip(NA_QROWS * a - NA_ROWS // 2, 0, rows - NA_KROWS)
    tok0 = pl.multiple_of(ks * GRID_W, GRID_W)
    nloc = NA_KROWS * NA_KCOLS
    nmb = GRID_W // NA_COLS
    for r in range(NA_KROWS):
        row0 = pl.multiple_of(tok0 + r * GRID_W, GRID_W)
        for src_ref, dst_ref in ((k_ref, kw_ref), (v_ref, vw_ref)):
            rowf = src_ref[pl.ds(row0, GRID_W), :].astype(F32)
            for m in range(nmb):
                dst_ref[m, r * NA_KCOLS:(r + 1) * NA_KCOLS, :] = (
                    rowf[NA_KSTART[m]:NA_KSTART[m] + NA_KCOLS, :].astype(BF16))

    def scores(i):
        p, m = divmod(i, nmb)
        cols = slice(p * LANES, (p + 1) * LANES)
        kcat = jnp.concatenate([kw_ref[m, :, cols], kc_ref[:, cols]], axis=0)
        vcat = jnp.concatenate([vw_ref[m, :, cols], vc_ref[:, cols]], axis=0)
        qm = jnp.concatenate(
            [q_ref[r * GRID_W + m * NA_COLS:r * GRID_W + (m + 1) * NA_COLS, cols]
             for r in range(NA_QROWS)], axis=0)
        zero = jnp.zeros((), BF16)
        q2 = jnp.concatenate([jnp.where(low, qm, zero), jnp.where(low, zero, qm)], axis=0)
        s = _dot_nt(q2, kcat)
        bias = jnp.concatenate([bias_ref[NA_COLVAR[m], 2 * p], bias_ref[NA_COLVAR[m], 2 * p + 1]], axis=0)
        return jnp.concatenate([s[:, :nloc] + bias, s[:, nloc:]], axis=1), vcat

    def values(i, sv):
        p, m = divmod(i, nmb)
        nq = NA_QROWS * NA_COLS
        o = _softmax_pv(sv[0], None, sv[1])
        o = jnp.where(low, o[:nq], o[nq:]).astype(BF16)
        for r in range(NA_QROWS):
            mix_ref[r * GRID_W + m * NA_COLS:r * GRID_W + (m + 1) * NA_COLS,
                    MIX_HALF + p * LANES:MIX_HALF + (p + 1) * LANES] = o[r * NA_COLS:(r + 1) * NA_COLS]

    _pipelined((MIX_HALF // LANES) * nmb, QK_AHEAD, scores, values)
    y = _dot(mix_ref[...], wout_ref[...])
    o_ref[...] = h_ref[...] + mod_ref[5:6, :] * y


def _odd_mixer(xp, q, k, v, kc, vc, bias_tbl, wpool, cscale, wout, mod, h):
    b, s, d = h.shape
    lc = kc.shape[1]
    tq = NA_QROWS * GRID_W
    nrb = s // tq
    halo = 8
    hb = tq // halo
    tok = lambda w: pl.BlockSpec((None, tq, w), lambda bi, i: (bi, i, 0))
    per_batch = lambda rows, w: _resident((None, rows, w), lambda bi, i: (bi, 0, 0))
    const = lambda shape: _resident(shape, lambda bi, i: (0,) * len(shape))

    def row_variant(i):
        return jnp.where(i == 0, 0, jnp.where(i == nrb - 1, 2, 1))

    return pl.pallas_call(
        functools.partial(_odd_mixer_kernel, seq=s),
        grid=(b, nrb),
        in_specs=[
            tok(MIX_HALF),
            pl.BlockSpec((None, halo, MIX_HALF), lambda bi, i: (bi, jnp.maximum(i * hb - 1, 0), 0)),
            pl.BlockSpec((None, halo, MIX_HALF), lambda bi, i: (bi, jnp.minimum((i + 1) * hb, s // halo - 1), 0)),
            tok(MIX_HALF),
            per_batch(s, MIX_HALF), per_batch(s, MIX_HALF),
            per_batch(lc, MIX_HALF), per_batch(lc, MIX_HALF),
            pl.BlockSpec((None,) + bias_tbl.shape[1:], lambda bi, i: (row_variant(i), 0, 0, 0, 0)),
            const(wpool.shape), const(cscale.shape), const((d, d)),
            pl.BlockSpec((None, N_MOD, d), lambda bi, i: (bi, 0, 0)),
            tok(d),
        ],
        out_specs=tok(d),
        out_shape=jax.ShapeDtypeStruct((b, s, d), F32),
        scratch_shapes=[pltpu.VMEM((tq, d), BF16), pltpu.VMEM((tq + 2 * halo, MIX_HALF), F32)]
        + [pltpu.VMEM((GRID_W // NA_COLS, NA_KROWS * NA_KCOLS, MIX_HALF), BF16)] * 2,
        compiler_params=_params("parallel", "arbitrary"),
        name="odd_mixer",
    )(xp, xp, xp, q, k, v, kc, vc, bias_tbl, wpool, cscale, wout, mod, h)


def _rope_tables(s):
    t = jnp.arange(s)
    row = (t // GRID_W).astype(F32)
    col = (t % GRID_W).astype(F32)
    m = HEAD_DIM // 4
    inv = 1.0 / (ROPE_THETA ** (jnp.arange(m, dtype=F32) / m))
    ang_r = row[:, None] * inv[None, :]
    ang_c = col[:, None] * inv[None, :]
    cos_h = jnp.concatenate([jnp.cos(ang_r)] * 2 + [jnp.cos(ang_c)] * 2, axis=-1)
    sin_h = jnp.concatenate([-jnp.sin(ang_r), jnp.sin(ang_r), -jnp.sin(ang_c), jnp.sin(ang_c)], axis=-1)
    reps = LANES // HEAD_DIM
    return jnp.tile(cos_h, (1, reps)), jnp.tile(sin_h, (1, reps))


def _block_diag_ones():
    idx = np.arange(MXU_DIM) // HEAD_DIM
    return jnp.asarray(idx[:, None] == idx[None, :], dtype=BF16)


def kernel(x, c, ctx, c_ctx, ada_w, ada_b, norm_g, ffn_w_gu, ffn_w_down, ev_w_in, ev_w_out,
           a_q_gain, a_k_gain, a_sink, b_v_gain, b_ws, b_bias, od_w_in, od_w_out, c_w_pool,
           c_scale, d_q_gain, d_k_gain, d_rpb):
    b, s, d = x.shape
    lc = ctx.shape[1]
    depth = ada_w.shape[0]
    assert d == D_MODEL and depth == 2 and s % TOK_TILE == 0 and s // GRID_W >= NA_KROWS
    assert lc % B_CHUNK == 0 and b <= 8

    mod_rows = 16
    cvec = jnp.zeros((mod_rows, d), F32).at[:b].set(c).at[b].set(c_ctx)
    mod = _modulation(cvec, ada_w, ada_b)
    lat_row = lambda bi: bi
    ctx_row = lambda bi: b

    ones = _block_diag_ones()
    cos, sin = _rope_tables(s)
    cos_id = jnp.ones((lc, LANES), F32)
    sin_id = jnp.zeros((lc, LANES), F32)
    tile_heads = lambda gain, n: jnp.tile(gain, n).reshape(1, n * HEAD_DIM)

    h, hc = x, ctx
    li, e = 0, 0
    wgu = ffn_w_gu[li].astype(BF16)
    wd = ffn_w_down[li].astype(BF16)
    h = _half_ffn(h, mod[li], lat_row, norm_g[li, 0], wgu[0], wd[0], 0)
    hc = _half_ffn(hc, mod[li], ctx_row, norm_g[li, 0], wgu[0], wd[0], 0)
    hsl = lambda hd: slice(hd * HEAD_DIM, (hd + 1) * HEAD_DIM)
    win = jnp.concatenate([ev_w_in[e][:, hsl(hd)] for hd in A_HEAD_PERM] + [ev_w_in[e][:, MIX_HALF:]],
                          axis=1).astype(BF16)
    wout = jnp.concatenate([ev_w_out[e][hsl(hd)] for hd in A_HEAD_PERM] + [ev_w_out[e][MIX_HALF:]],
                           axis=0).astype(BF16)
    qg = tile_heads(a_q_gain[e], N_HEADS)
    kg = tile_heads(a_k_gain[e], A_KV_HEADS)
    vg = b_v_gain[e].reshape(1, MIX_HALF)
    q, k, v, u, vv = _even_inproj(h, mod[li], lat_row, norm_g[li, 1], win, qg, kg, vg, cos, sin, ones)
    qc, kc, vc, uc, vvc = _even_inproj(hc, mod[li], ctx_row, norm_g[li, 1], win, qg, kg, vg,
                                       cos_id, sin_id, ones)
    sink_cols = jnp.repeat(a_sink[e] * LOG2E, A_BLOCK).reshape(N_HEADS * A_BLOCK, 1)
    ws = b_ws[e].astype(BF16)
    gb = jnp.repeat(b_bias[e].T, HEAD_DIM, axis=1)
    h = _even_mixer(q, k, v, kc, vc, sink_cols, u, vv, ws, gb, wout, mod[li], lat_row, h, True)
    hc = _even_mixer(qc, None, None, kc, vc, sink_cols, uc, vvc, ws, gb, wout, mod[li], ctx_row, hc, False)
    h = _half_ffn(h, mod[li], lat_row, norm_g[li, 2], wgu[1], wd[1], 6)
    hc = _half_ffn(hc, mod[li], ctx_row, norm_g[li, 2], wgu[1], wd[1], 6)

    li, o = 1, 0
    wgu = ffn_w_gu[li].astype(BF16)
    wd = ffn_w_down[li].astype(BF16)
    h = _half_ffn(h, mod[li], lat_row, norm_g[li, 0], wgu[0], wd[0], 0)
    hc = _half_ffn(hc, mod[li], ctx_row, norm_g[li, 0], wgu[0], wd[0], 0)
    win = od_w_in[o].astype(BF16)
    wout = od_w_out[o].astype(BF16)
    qg = tile_heads(d_q_gain[o], N_HEADS)
    kg = tile_heads(d_k_gain[o], N_HEADS)
    xp, q, k, v = _odd_inproj(h, mod[li], lat_row, norm_g[li, 1], win, qg, kg, ones, False)
    kc, vc = _odd_inproj(hc, mod[li], ctx_row, norm_g[li, 1], win, qg, kg, ones, True)
    bias_tbl = _na_bias_table(d_rpb[o], s // GRID_W)
    h = _odd_mixer(xp, q, k, v, kc, vc, bias_tbl, c_w_pool[o].astype(BF16), c_scale[o].reshape(1, MIX_HALF),
                   wout, mod[li], h)
    h = _half_ffn(h, mod[li], lat_row, norm_g[li, 2], wgu[1], wd[1], 6)
    return h
```

```python
import functools

import numpy as np
import jax
import jax.numpy as jnp
from jax import lax
from jax.experimental import pallas as pl
from jax.experimental.pallas import tpu as pltpu

F32 = jnp.float32
BF16 = jnp.bfloat16

D_MODEL = 1024
GRID_W = 64
HEAD_DIM = 64
MIX_HALF = D_MODEL // 2
N_MOD = 9
D_FF = ((8 * D_MODEL // 3 + 127) // 128) * 128
EPS = 1e-6
ROPE_THETA = 10000.0
NEG = -1e30
N_HEADS = MIX_HALF // HEAD_DIM
A_KV_HEADS = N_HEADS // 4
A_KV = A_KV_HEADS * HEAD_DIM
WINDOW = 128
A_BLOCK = 128
B_CHUNK = 128
EVEN_IN = MIX_HALF + 2 * A_KV + 2 * MIX_HALF
POOL_WINDOWS = (2, 4, 8, 16)
C_GROUP_W = MIX_HALF // len(POOL_WINDOWS)
NA_ROWS = 8
NA_COLS = 16
ODD_IN = 4 * MIX_HALF

LANES = 128
MXU_DIM = 256
VMEM_LIMIT = 56 * 1024 * 1024

LOG2E = 1.4426950408889634
Q_SCALE = HEAD_DIM ** -0.5 * LOG2E
QK_AHEAD = 2
TOK_TILE = 512
FFN_TILE = 1024
FFN_SUB = 512
EVEN_TQ = 512
NA_QROWS = 8
NA_KROWS = 16
NA_KCOLS = 2 * NA_COLS
NA_KSTART = tuple(int(v) for v in np.clip(np.arange(GRID_W // NA_COLS) * NA_COLS - NA_COLS // 2,
                                         0, GRID_W - NA_KCOLS))
NA_COLVAR = (0, 1, 1, 2)
FFN_CHUNKS = (512, 512, 512, 512, 512, 256)
assert sum(FFN_CHUNKS) == D_FF

A_HEAD_PERM = (0, 4, 1, 5, 2, 6, 3, 7)


def _params(*sem):
    return pltpu.CompilerParams(dimension_semantics=sem, vmem_limit_bytes=VMEM_LIMIT)


def _resident(shape, index_map):
    return pl.BlockSpec(shape, index_map, pipeline_mode=pl.Buffered(1))


def _norm_mod(h, g, shift, scale):
    y = h * lax.rsqrt(jnp.mean(h * h, axis=-1, keepdims=True) + EPS)
    return (y * g) * (1.0 + scale) + shift


def _dot(a, b):
    return jnp.dot(a, b, preferred_element_type=F32)


def _dot_nt(a, b):
    return lax.dot_general(a, b, (((1,), (1,)), ((), ())), preferred_element_type=F32)


def _head_rms(x, gain, ones_bd):
    w = x.shape[1]
    step = min(w, MXU_DIM)
    ones = ones_bd[:step, :step]
    parts = []
    for c in range(0, w, step):
        xs = x[:, c:c + step]
        ss = _dot((xs * xs).astype(BF16), ones)
        parts.append(xs * lax.rsqrt(ss / HEAD_DIM + EPS))
    y = parts[0] if len(parts) == 1 else jnp.concatenate(parts, axis=1)
    return y * gain


def _rope(x, cos, sin_signed):
    lane = lax.broadcasted_iota(jnp.int32, (1, LANES), 1)
    first = (lane & 16) == 0
    parts = []
    for c in range(0, x.shape[1], LANES):
        xb = x[:, c:c + LANES]
        partner = jnp.where(first, pltpu.roll(xb, LANES - 16, 1), pltpu.roll(xb, 16, 1))
        parts.append(xb * cos + partner * sin_signed)
    return parts[0] if len(parts) == 1 else jnp.concatenate(parts, axis=1)


def _softmax_pv(s, vcat, sink=None):
    m = jnp.max(s, axis=-1, keepdims=True)
    if sink is not None:
        m = jnp.maximum(m, sink)
    p = jnp.exp2(s - m)
    den = jnp.sum(p, axis=-1, keepdims=True)
    if sink is not None:
        den = den + jnp.exp2(sink - m)
    return _dot(p.astype(BF16), vcat) / den


def _pipelined(n, ahead, stage_a, stage_b):
    pending = [stage_a(i) for i in range(min(ahead, n))]
    for i in range(n):
        if i + ahead < n:
            pending.append(stage_a(i + ahead))
        stage_b(i, pending.pop(0))


def _mod_kernel(c_ref, w_ref, b_ref, o_ref):
    sc = jax.nn.silu(c_ref[...])
    o_ref[...] = _dot(sc, w_ref[...]) + b_ref[...]


def _modulation(cvec, ada_w, ada_b):
    depth, d, n = ada_w.shape
    r = cvec.shape[0]
    tn = 3072
    out = pl.pallas_call(
        _mod_kernel,
        grid=(depth, n // tn),
        in_specs=[
            pl.BlockSpec((r, d), lambda l, j: (0, 0)),
            pl.BlockSpec((None, d, tn), lambda l, j: (l, 0, j)),
            pl.BlockSpec((None, 1, tn), lambda l, j: (l, 0, j)),
        ],
        out_specs=pl.BlockSpec((None, r, tn), lambda l, j: (l, 0, j)),
        out_shape=jax.ShapeDtypeStruct((depth, r, n), F32),
        compiler_params=_params("parallel", "parallel"),
        name="adaln_mod",
    )(cvec, ada_w, ada_b.reshape(depth, 1, n))
    return out.reshape(depth, r, N_MOD, d)


def _ffn_kernel(h_ref, mod_ref, g_ref, wgu_ref, wd_ref, o_ref, z_ref, acc_ref, *, mrow):
    sub = z_ref.shape[1]
    nsub = h_ref.shape[0] // sub
    slot = lambda t: lax.rem(pl.program_id(1) + t, 2)

    def norm(t):
        h = h_ref[t * sub:(t + 1) * sub, :]
        z_ref[slot(t)] = _norm_mod(h, g_ref[...], mod_ref[mrow:mrow + 1, :],
                                   mod_ref[mrow + 1:mrow + 2, :]).astype(BF16)

    def residual(t):
        rows = slice(t * sub, (t + 1) * sub)
        o_ref[rows, :] = h_ref[rows, :] + (0.5 * mod_ref[mrow + 2:mrow + 3, :]) * acc_ref[slot(t)]

    norm(0)
    for t in range(nsub):
        acc = None
        off = 0
        for ci, ch in enumerate(FFN_CHUNKS):
            z = z_ref[slot(t)]
            gt = _dot(z, wgu_ref[:, off:off + ch])
            up = _dot(z, wgu_ref[:, D_FF + off:D_FF + off + ch])
            a = (jax.nn.silu(gt) * up).astype(BF16)
            part = _dot(a, wd_ref[off:off + ch, :])
            acc = part if acc is None else acc + part
            off += ch
            if ci == 0 and t >= 1:
                residual(t - 1)
            if ci == 1 and t + 1 < nsub:
                norm(t + 1)
        acc_ref[slot(t)] = acc
    residual(nsub - 1)


def _half_ffn(h, mod, mod_row_of_batch, g, wgu, wd, li, half):
    b, s, d = h.shape
    tm = min(FFN_TILE, s)
    mrow = 6 * half
    return pl.pallas_call(
        functools.partial(_ffn_kernel, mrow=mrow),
        grid=(b, s // tm),
        in_specs=[
            pl.BlockSpec((None, tm, d), lambda bi, i: (bi, i, 0)),
            pl.BlockSpec((None, N_MOD, d), lambda bi, i: (mod_row_of_batch(bi), 0, 0)),
            _resident((1, d), lambda bi, i: (0, 0)),
            _resident((None, None, d, 2 * D_FF), lambda bi, i: (li, half, 0, 0)),
            _resident((None, None, D_FF, d), lambda bi, i: (li, half, 0, 0)),
        ],
        out_specs=pl.BlockSpec((None, tm, d), lambda bi, i: (bi, i, 0)),
        out_shape=jax.ShapeDtypeStruct((b, s, d), F32),
        scratch_shapes=[pltpu.VMEM((2, min(FFN_SUB, tm), d), BF16), pltpu.VMEM((2, min(FFN_SUB, tm), d), F32)],
        compiler_params=_params("parallel", "parallel"),
        name="half_ffn",
    )(h, mod, g.reshape(1, d), wgu, wd)


def _even_inproj_kernel(h_ref, mod_ref, g_ref, win_ref, qg_ref, kg_ref, vg_ref, cos_ref, sin_ref,
                        ones_ref, q_ref, k_ref, v_ref, u_ref, vv_ref):
    z = _norm_mod(h_ref[...], g_ref[...], mod_ref[3:4, :], mod_ref[4:5, :]).astype(BF16)
    ones = ones_ref[...]
    cos = cos_ref[...]
    sin = sin_ref[...]
    cq, ck, cv, cu, cb = 0, MIX_HALF, MIX_HALF + A_KV, MIX_HALF + 2 * A_KV, 2 * MIX_HALF + 2 * A_KV
    q = _dot(z, win_ref[:, cq:cq + MIX_HALF])
    k = _dot(z, win_ref[:, ck:ck + A_KV])
    q = _rope(_head_rms(q, qg_ref[...], ones), cos, sin)
    q_ref[...] = (q * Q_SCALE).astype(BF16)
    bv = jax.nn.gelu(_dot(z, win_ref[:, cb:cb + MIX_HALF]))
    k_ref[...] = _rope(_head_rms(k, kg_ref[...], ones), cos, sin).astype(BF16)
    v_ref[...] = _dot(z, win_ref[:, cv:cv + A_KV]).astype(BF16)
    u = _dot(z, win_ref[:, cu:cu + MIX_HALF])
    vv_ref[...] = _head_rms(bv, vg_ref[...], ones).astype(BF16)
    u_ref[...] = jax.nn.gelu(u)


def _even_inproj(h, mod, mod_row_of_batch, g, win, qg, kg, vg, cos, sin, ones):
    b, s, d = h.shape
    tm = min(TOK_TILE, s)
    tok = lambda w: pl.BlockSpec((None, tm, w), lambda bi, i: (bi, i, 0))
    const = lambda shape: _resident(shape, lambda bi, i: (0,) * len(shape))
    return pl.pallas_call(
        _even_inproj_kernel,
        grid=(b, s // tm),
        in_specs=[
            tok(d),
            pl.BlockSpec((None, N_MOD, d), lambda bi, i: (mod_row_of_batch(bi), 0, 0)),
            const((1, d)), const((d, EVEN_IN)), const((1, MIX_HALF)), const((1, A_KV)),
            const((1, MIX_HALF)),
            pl.BlockSpec((tm, LANES), lambda bi, i: (i, 0)),
            pl.BlockSpec((tm, LANES), lambda bi, i: (i, 0)),
            const((MXU_DIM, MXU_DIM)),
        ],
        out_specs=[tok(MIX_HALF), tok(A_KV), tok(A_KV), tok(MIX_HALF), tok(MIX_HALF)],
        out_shape=[
            jax.ShapeDtypeStruct((b, s, MIX_HALF), BF16),
            jax.ShapeDtypeStruct((b, s, A_KV), BF16),
            jax.ShapeDtypeStruct((b, s, A_KV), BF16),
            jax.ShapeDtypeStruct((b, s, MIX_HALF), F32),
            jax.ShapeDtypeStruct((b, s, MIX_HALF), BF16),
        ],
        compiler_params=_params("parallel", "parallel"),
        name="even_inproj",
    )(h, mod, g.reshape(1, d), win, qg, kg, vg, cos, sin, ones)


def _even_mixer_kernel(*refs, local, seq):
    if local:
        (q_ref, k_ref, v_ref, kc_ref, vc_ref, sink_ref, u_ref, vv_ref, ws_ref, gb_ref, wout_ref,
         mod_ref, h_ref, o_ref, mix_ref) = refs
    else:
        (q_ref, kc_ref, vc_ref, sink_ref, u_ref, vv_ref, ws_ref, gb_ref, wout_ref,
         mod_ref, h_ref, o_ref, mix_ref) = refs
    tq = q_ref.shape[0]
    nblk = tq // A_BLOCK
    nloc = 3 * A_BLOCK
    lane = lax.broadcasted_iota(jnp.int32, (1, LANES), 1)
    low = lane < HEAD_DIM
    kc = kc_ref[...]
    vc = vc_ref[...]
    groups = N_HEADS // A_KV_HEADS

    def scores(blk):
        r0 = blk * A_BLOCK
        vvb = vv_ref[r0:r0 + B_CHUNK, :]
        for p in range(MIX_HALF // LANES):
            cols = slice(p * LANES, (p + 1) * LANES)
            vsl = vvb[:, cols]
            mixed = jnp.where(low, _dot(ws_ref[2 * p], vsl), _dot(ws_ref[2 * p + 1], vsl)) + gb_ref[:, cols]
            mix_ref[r0:r0 + B_CHUNK, MIX_HALF + p * LANES:MIX_HALF + (p + 1) * LANES] = (
                u_ref[r0:r0 + B_CHUNK, cols] * mixed).astype(BF16)
        qb = q_ref[r0:r0 + A_BLOCK, :]
        q8 = jnp.concatenate(
            [jnp.where(low if j == 0 else jnp.logical_not(low), qb[:, p * LANES:(p + 1) * LANES],
                       jnp.zeros((), BF16))
             for j in range(A_KV_HEADS) for p in range(groups)], axis=0)
        if not local:
            return _dot_nt(q8, kc), vc
        n = pl.program_id(1) * nblk + blk
        start = pl.multiple_of(jnp.clip((n - 1) * A_BLOCK, 0, seq - nloc), A_BLOCK)
        kcat = jnp.concatenate([k_ref[pl.ds(start, nloc), :], kc], axis=0)
        vcat = jnp.concatenate([v_ref[pl.ds(start, nloc), :], vc], axis=0)
        qpos = n * A_BLOCK + lax.broadcasted_iota(jnp.int32, (A_BLOCK, 1), 0)
        kpos = start + lax.broadcasted_iota(jnp.int32, (1, nloc), 1)
        mask = jnp.where(jnp.abs(qpos - kpos) <= WINDOW, 0.0, NEG)
        s = _dot_nt(q8, kcat)
        s_loc = (s[:, :nloc].reshape(N_HEADS, A_BLOCK, nloc) + mask[None]).reshape(N_HEADS * A_BLOCK, nloc)
        return jnp.concatenate([s_loc, s[:, nloc:]], axis=1), vcat

    def values(blk, sv):
        r0 = blk * A_BLOCK
        o = _softmax_pv(sv[0], sv[1], sink_ref[...])
        for p in range(groups):
            lo_rows = o[p * A_BLOCK:(p + 1) * A_BLOCK]
            hi_rows = o[(groups + p) * A_BLOCK:(groups + p + 1) * A_BLOCK]
            mix_ref[r0:r0 + A_BLOCK, p * LANES:(p + 1) * LANES] = jnp.where(low, lo_rows, hi_rows).astype(BF16)

    _pipelined(nblk, QK_AHEAD, scores, values)
    y = _dot(mix_ref[...], wout_ref[...])
    o_ref[...] = h_ref[...] + mod_ref[5:6, :] * y


def _even_mixer(q, k, v, kc, vc, sink_cols, u, vv, ws, gb, wout, mod, mod_row_of_batch, h, local):
    b, s, d = h.shape
    lc = kc.shape[1]
    tq = min(EVEN_TQ, s)
    tok = lambda w: pl.BlockSpec((None, tq, w), lambda bi, i: (bi, i, 0))
    per_batch = lambda rows, w: pl.BlockSpec((None, rows, w), lambda bi, i: (bi, 0, 0))
    const = lambda shape: _resident(shape, lambda bi, i: (0,) * len(shape))
    in_specs = [tok(MIX_HALF)]
    args = [q]
    if local:
        in_specs += [per_batch(s, A_KV), per_batch(s, A_KV)]
        args += [k, v]
    in_specs += [
        per_batch(lc, A_KV), per_batch(lc, A_KV),
        const(sink_cols.shape),
        tok(MIX_HALF), tok(MIX_HALF),
        const(ws.shape), const(gb.shape), const((d, d)),
        pl.BlockSpec((None, N_MOD, d), lambda bi, i: (mod_row_of_batch(bi), 0, 0)),
        tok(d),
    ]
    args += [kc, vc, sink_cols, u, vv, ws, gb, wout, mod, h]
    return pl.pallas_call(
        functools.partial(_even_mixer_kernel, local=local, seq=s),
        grid=(b, s // tq),
        in_specs=in_specs,
        out_specs=tok(d),
        out_shape=jax.ShapeDtypeStruct((b, s, d), F32),
        scratch_shapes=[pltpu.VMEM((tq, d), BF16)],
        compiler_params=_params("parallel", "parallel"),
        name="even_mixer" if local else "even_mixer_ctx",
    )(*args)


def _odd_inproj_kernel(h_ref, mod_ref, g_ref, win_ref, qg_ref, kg_ref, ones_ref, *out_refs, kv_only):
    z = _norm_mod(h_ref[...], g_ref[...], mod_ref[3:4, :], mod_ref[4:5, :]).astype(BF16)
    ones = ones_ref[...]
    w = MIX_HALF
    if kv_only:
        k_ref, v_ref = out_refs
        k = _dot(z, win_ref[:, 2 * w:3 * w])
    else:
        xp_ref, q_ref, k_ref, v_ref = out_refs
        q = _dot(z, win_ref[:, w:2 * w])
        k = _dot(z, win_ref[:, 2 * w:3 * w])
        q = _head_rms(q, qg_ref[...], ones)
        q_ref[...] = (q * Q_SCALE).astype(BF16)
        xp_ref[...] = _dot(z, win_ref[:, 0:w])
    v = _dot(z, win_ref[:, 3 * w:4 * w])
    k_ref[...] = _head_rms(k, kg_ref[...], ones).astype(BF16)
    v_ref[...] = v.astype(BF16)


def _odd_inproj(h, mod, mod_row_of_batch, g, win, qg, kg, ones, kv_only):
    b, s, d = h.shape
    tm = min(TOK_TILE, s)
    tok = lambda w: pl.BlockSpec((None, tm, w), lambda bi, i: (bi, i, 0))
    const = lambda shape: _resident(shape, lambda bi, i: (0,) * len(shape))
    bf = jax.ShapeDtypeStruct((b, s, MIX_HALF), BF16)
    if kv_only:
        out_specs, out_shape = [tok(MIX_HALF)] * 2, [bf, bf]
    else:
        out_specs = [tok(MIX_HALF)] * 4
        out_shape = [jax.ShapeDtypeStruct((b, s, MIX_HALF), F32), bf, bf, bf]
    return pl.pallas_call(
        functools.partial(_odd_inproj_kernel, kv_only=kv_only),
        grid=(b, s // tm),
        in_specs=[
            tok(d),
            pl.BlockSpec((None, N_MOD, d), lambda bi, i: (mod_row_of_batch(bi), 0, 0)),
            const((1, d)), const((d, ODD_IN)), const((1, MIX_HALF)), const((1, MIX_HALF)),
            const((MXU_DIM, MXU_DIM)),
        ],
        out_specs=out_specs,
        out_shape=out_shape,
        compiler_params=_params("parallel", "parallel"),
        name="odd_inproj_kv" if kv_only else "odd_inproj",
    )(h, mod, g.reshape(1, d), win, qg, kg, ones)


def _na_tables(rows):
    nrb = rows // NA_QROWS
    nc = 2 * NA_COLS - 1
    ridx = np.zeros((3, NA_QROWS, NA_KROWS), np.int32)
    for rv, a in enumerate((0, nrb // 2, nrb - 1)):
        ks = int(np.clip(NA_QROWS * a - NA_ROWS // 2, 0, rows - NA_KROWS))
        r = NA_QROWS * a + np.arange(NA_QROWS)[:, None]
        kr = ks + np.arange(NA_KROWS)[None, :]
        r0 = np.clip(r - NA_ROWS // 2, 0, rows - NA_ROWS)
        ridx[rv] = np.where((kr >= r0) & (kr < r0 + NA_ROWS), kr - r + NA_ROWS - 1, -1)
    csel = np.zeros((3, NA_COLS, NA_KCOLS, nc), np.float32)
    cvalid = np.zeros((3, NA_COLS, NA_KCOLS), bool)
    for cv, m in enumerate((0, 1, 3)):
        qc = NA_COLS * m + np.arange(NA_COLS)[:, None]
        kcol = NA_KSTART[m] + np.arange(NA_KCOLS)[None, :]
        qstart = np.clip(qc - NA_COLS // 2, 0, GRID_W - NA_COLS)
        cvalid[cv] = (kcol >= qstart) & (kcol < qstart + NA_COLS)
        ci = np.clip(kcol - qc, 1 - NA_COLS, NA_COLS - 1) + NA_COLS - 1
        csel[cv] = np.arange(nc) == ci[..., None]
    return ridx, csel, cvalid


def _na_bias_table(rpb, rows):
    ridx, csel, cvalid = _na_tables(rows)
    cb = jnp.einsum("hab,cqjb->chaqj", rpb, csel, precision=lax.Precision.HIGHEST)
    cb = jnp.where(cvalid[:, None, None], cb * LOG2E, NEG)
    outside = jnp.full(cb.shape[:2] + cb.shape[3:], NEG, F32)
    variants = []
    for rv in range(ridx.shape[0]):
        strips = []
        for rl in range(NA_QROWS):
            blocks = [cb[:, :, int(ridx[rv, rl, kl])] if ridx[rv, rl, kl] >= 0 else outside
                      for kl in range(NA_KROWS)]
            strips.append(jnp.concatenate(blocks, axis=-1))
        variants.append(jnp.stack(strips, axis=2))
    tbl = jnp.stack(variants, axis=0)
    return tbl.reshape(3, 3, rpb.shape[0], NA_QROWS * NA_COLS, NA_KROWS * NA_KCOLS)


def _odd_mixer_kernel(xp_ref, xprev_ref, xnext_ref, q_ref, k_ref, v_ref, kc_ref, vc_ref, bias_ref,
                      wpool_ref, cs_ref, wout_ref, mod_ref, h_ref, o_ref, mix_ref, xw_ref, kw_ref, vw_ref,
                      *, seq):
    a = pl.program_id(1)
    last = pl.num_programs(1) - 1
    tq = q_ref.shape[0]
    halo = xprev_ref.shape[0]
    lane = lax.broadcasted_iota(jnp.int32, (1, LANES), 1)
    low = lane < HEAD_DIM

    xw_ref[0:halo, :] = jnp.where(a == 0, 0.0, xprev_ref[...])
    xw_ref[halo:halo + tq, :] = xp_ref[...]
    xw_ref[halo + tq:halo + tq + halo, :] = jnp.where(a == last, 0.0, xnext_ref[...])
    t = a * tq + lax.broadcasted_iota(jnp.int32, (tq, 1), 0)

    def pool(gi):
        w = POOL_WINDOWS[gi]
        cols = slice(gi * C_GROUP_W, (gi + 1) * C_GROUP_W)
        acc = None
        for dlt in range(-(w // 2), w - w // 2):
            term = xw_ref[halo + dlt:halo + dlt + tq, cols]
            acc = term if acc is None else acc + term
        cnt = (jnp.minimum(t + (w - w // 2), seq) - jnp.maximum(t - w // 2, 0)).astype(F32)
        pooled = acc / cnt - xp_ref[:, cols]
        y = _dot(pooled.astype(BF16), wpool_ref[gi]) * cs_ref[:, cols]
        mix_ref[:, cols] = y.astype(BF16)

    rows = seq // GRID_W
    ks = jnp.clip(NA_QROWS * a - NA_ROWS // 2, 0, rows - NA_KROWS)
    tok0 = pl.multiple_of(ks * GRID_W, GRID_W)
    nmb = GRID_W // NA_COLS
    for r in range(NA_KROWS):
        row0 = pl.multiple_of(tok0 + r * GRID_W, GRID_W)
        for src_ref, dst_ref in ((k_ref, kw_ref), (v_ref, vw_ref)):
            rowf = src_ref[pl.ds(row0, GRID_W), :].astype(F32)
            for m in range(nmb):
                dst_ref[m, r * NA_KCOLS:(r + 1) * NA_KCOLS, :] = (
                    rowf[NA_KSTART[m]:NA_KSTART[m] + NA_KCOLS, :].astype(BF16))

    def scores(i):
        p, m = divmod(i, nmb)
        cols = slice(p * LANES, (p + 1) * LANES)
        kcat = jnp.concatenate([kw_ref[m, :, cols], kc_ref[:, cols]], axis=0)
        vcat = jnp.concatenate([vw_ref[m, :, cols], vc_ref[:, cols]], axis=0)
        qm = jnp.concatenate(
            [q_ref[r * GRID_W + m * NA_COLS:r * GRID_W + (m + 1) * NA_COLS, cols]
             for r in range(NA_QROWS)], axis=0)
        zero = jnp.zeros((), BF16)
        q2 = jnp.concatenate([jnp.where(low, qm, zero), jnp.where(low, zero, qm)], axis=0)
        s = _dot_nt(q2, kcat)
        nloc = kw_ref.shape[1]
        bias = jnp.concatenate([bias_ref[NA_COLVAR[m], 2 * p], bias_ref[NA_COLVAR[m], 2 * p + 1]], axis=0)
        return jnp.concatenate([s[:, :nloc] + bias, s[:, nloc:]], axis=1), vcat

    def values(i, sv):
        p, m = divmod(i, nmb)
        nq = NA_QROWS * NA_COLS
        o = _softmax_pv(sv[0], sv[1])
        o = jnp.where(low, o[:nq], o[nq:]).astype(BF16)
        for r in range(NA_QROWS):
            mix_ref[r * GRID_W + m * NA_COLS:r * GRID_W + (m + 1) * NA_COLS,
                    MIX_HALF + p * LANES:MIX_HALF + (p + 1) * LANES] = o[r * NA_COLS:(r + 1) * NA_COLS]

    for gi in range(len(POOL_WINDOWS)):
        pool(gi)
    _pipelined((MIX_HALF // LANES) * nmb, QK_AHEAD, scores, values)
    y = _dot(mix_ref[...], wout_ref[...])
    o_ref[...] = h_ref[...] + mod_ref[5:6, :] * y


def _odd_mixer(xp, q, k, v, kc, vc, bias_tbl, wpool, cscale, wout, mod, h):
    b, s, d = h.shape
    lc = kc.shape[1]
    tq = NA_QROWS * GRID_W
    nrb = s // tq
    halo = 8
    hb = tq // halo
    tok = lambda w: pl.BlockSpec((None, tq, w), lambda bi, i: (bi, i, 0))
    per_batch = lambda rows, w: _resident((None, rows, w), lambda bi, i: (bi, 0, 0))
    const = lambda shape: _resident(shape, lambda bi, i: (0,) * len(shape))

    def row_variant(i):
        return jnp.where(i == 0, 0, jnp.where(i == nrb - 1, 2, 1))

    return pl.pallas_call(
        functools.partial(_odd_mixer_kernel, seq=s),
        grid=(b, nrb),
        in_specs=[
            tok(MIX_HALF),
            pl.BlockSpec((None, halo, MIX_HALF), lambda bi, i: (bi, jnp.maximum(i * hb - 1, 0), 0)),
            pl.BlockSpec((None, halo, MIX_HALF), lambda bi, i: (bi, jnp.minimum((i + 1) * hb, s // halo - 1), 0)),
            tok(MIX_HALF),
            per_batch(s, MIX_HALF), per_batch(s, MIX_HALF),
            per_batch(lc, MIX_HALF), per_batch(lc, MIX_HALF),
            pl.BlockSpec((None,) + bias_tbl.shape[1:], lambda bi, i: (row_variant(i), 0, 0, 0, 0)),
            const(wpool.shape), const(cscale.shape), const((d, d)),
            pl.BlockSpec((None, N_MOD, d), lambda bi, i: (bi, 0, 0)),
            tok(d),
        ],
        out_specs=tok(d),
        out_shape=jax.ShapeDtypeStruct((b, s, d), F32),
        scratch_shapes=[pltpu.VMEM((tq, d), BF16), pltpu.VMEM((tq + 2 * halo, MIX_HALF), F32)]
        + [pltpu.VMEM((GRID_W // NA_COLS, NA_KROWS * NA_KCOLS, MIX_HALF), BF16)] * 2,
        compiler_params=_params("parallel", "arbitrary"),
        name="odd_mixer",
    )(xp, xp, xp, q, k, v, kc, vc, bias_tbl, wpool, cscale, wout, mod, h)


def _rope_tables(s):
    t = jnp.arange(s)
    row = (t // GRID_W).astype(F32)
    col = (t % GRID_W).astype(F32)
    m = HEAD_DIM // 4
    inv = 1.0 / (ROPE_THETA ** (jnp.arange(m, dtype=F32) / m))
    ang_r = row[:, None] * inv[None, :]
    ang_c = col[:, None] * inv[None, :]
    cos_h = jnp.concatenate([jnp.cos(ang_r)] * 2 + [jnp.cos(ang_c)] * 2, axis=-1)
    sin_h = jnp.concatenate([-jnp.sin(ang_r), jnp.sin(ang_r), -jnp.sin(ang_c), jnp.sin(ang_c)], axis=-1)
    reps = LANES // HEAD_DIM
    return jnp.tile(cos_h, (1, reps)), jnp.tile(sin_h, (1, reps))


def _block_diag_ones():
    idx = np.arange(MXU_DIM) // HEAD_DIM
    return jnp.asarray(idx[:, None] == idx[None, :], dtype=BF16)


def kernel(x, c, ctx, c_ctx, ada_w, ada_b, norm_g, ffn_w_gu, ffn_w_down, ev_w_in, ev_w_out,
           a_q_gain, a_k_gain, a_sink, b_v_gain, b_ws, b_bias, od_w_in, od_w_out, c_w_pool,
           c_scale, d_q_gain, d_k_gain, d_rpb):
    b, s, d = x.shape
    lc = ctx.shape[1]
    depth = ada_w.shape[0]
    assert d == D_MODEL and depth == 2 and s % TOK_TILE == 0 and s // GRID_W >= NA_KROWS
    assert lc % B_CHUNK == 0 and b <= 8

    mod_rows = 16
    cvec = jnp.zeros((mod_rows, d), F32).at[:b].set(c).at[b].set(c_ctx)
    mod = _modulation(cvec, ada_w, ada_b)
    lat_row = lambda bi: bi
    ctx_row = lambda bi: b

    ones = _block_diag_ones()
    cos, sin = _rope_tables(s)
    cos_id = jnp.ones((lc, LANES), F32)
    sin_id = jnp.zeros((lc, LANES), F32)
    tile_heads = lambda gain, n: jnp.tile(gain, n).reshape(1, n * HEAD_DIM)

    wgu = ffn_w_gu.astype(BF16)
    wd = ffn_w_down.astype(BF16)
    h, hc = x, ctx
    li, e = 0, 0
    h = _half_ffn(h, mod[li], lat_row, norm_g[li, 0], wgu, wd, li, 0)
    hc = _half_ffn(hc, mod[li], ctx_row, norm_g[li, 0], wgu, wd, li, 0)
    hsl = lambda hd: slice(hd * HEAD_DIM, (hd + 1) * HEAD_DIM)
    win = jnp.concatenate([ev_w_in[e][:, hsl(hd)] for hd in A_HEAD_PERM] + [ev_w_in[e][:, MIX_HALF:]],
                          axis=1).astype(BF16)
    wout = jnp.concatenate([ev_w_out[e][hsl(hd)] for hd in A_HEAD_PERM] + [ev_w_out[e][MIX_HALF:]],
                           axis=0).astype(BF16)
    qg = tile_heads(a_q_gain[e], N_HEADS)
    kg = tile_heads(a_k_gain[e], A_KV_HEADS)
    vg = b_v_gain[e].reshape(1, MIX_HALF)
    q, k, v, u, vv = _even_inproj(h, mod[li], lat_row, norm_g[li, 1], win, qg, kg, vg, cos, sin, ones)
    qc, kc, vc, uc, vvc = _even_inproj(hc, mod[li], ctx_row, norm_g[li, 1], win, qg, kg, vg,
                                       cos_id, sin_id, ones)
    sink_cols = jnp.repeat(a_sink[e] * LOG2E, A_BLOCK).reshape(N_HEADS * A_BLOCK, 1)
    ws = b_ws[e].astype(BF16)
    gb = jnp.repeat(b_bias[e].T, HEAD_DIM, axis=1)
    h = _even_mixer(q, k, v, kc, vc, sink_cols, u, vv, ws, gb, wout, mod[li], lat_row, h, True)
    hc = _even_mixer(qc, None, None, kc, vc, sink_cols, uc, vvc, ws, gb, wout, mod[li], ctx_row, hc, False)
    h = _half_ffn(h, mod[li], lat_row, norm_g[li, 2], wgu, wd, li, 1)
    hc = _half_ffn(hc, mod[li], ctx_row, norm_g[li, 2], wgu, wd, li, 1)

    li, o = 1, 0
    h = _half_ffn(h, mod[li], lat_row, norm_g[li, 0], wgu, wd, li, 0)
    hc = _half_ffn(hc, mod[li], ctx_row, norm_g[li, 0], wgu, wd, li, 0)
    win = od_w_in[o].astype(BF16)
    wout = od_w_out[o].astype(BF16)
    qg = tile_heads(d_q_gain[o], N_HEADS)
    kg = tile_heads(d_k_gain[o], N_HEADS)
    xp, q, k, v = _odd_inproj(h, mod[li], lat_row, norm_g[li, 1], win, qg, kg, ones, False)
    kc, vc = _odd_inproj(hc, mod[li], ctx_row, norm_g[li, 1], win, qg, kg, ones, True)
    bias_tbl = _na_bias_table(d_rpb[o], s // GRID_W)
    h = _odd_mixer(xp, q, k, v, kc, vc, bias_tbl, c_w_pool[o].astype(BF16), c_scale[o].reshape(1, MIX_HALF),
                   wout, mod[li], h)
    h = _half_ffn(h, mod[li], lat_row, norm_g[li, 2], wgu, wd, li, 1)
    return h
```

```python
import functools

import numpy as np
import jax
import jax.numpy as jnp
from jax import lax
from jax.experimental import pallas as pl
from jax.experimental.pallas import tpu as pltpu

F32 = jnp.float32
BF16 = jnp.bfloat16

D_MODEL = 1024
GRID_W = 64
HEAD_DIM = 64
MIX_HALF = D_MODEL // 2
N_MOD = 9
D_FF = ((8 * D_MODEL // 3 + 127) // 128) * 128
EPS = 1e-6
ROPE_THETA = 10000.0
NEG = -1e30
N_HEADS = MIX_HALF // HEAD_DIM
A_KV_HEADS = N_HEADS // 4
A_KV = A_KV_HEADS * HEAD_DIM
WINDOW = 128
A_BLOCK = 128
B_CHUNK = 128
EVEN_IN = MIX_HALF + 2 * A_KV + 2 * MIX_HALF
POOL_WINDOWS = (2, 4, 8, 16)
C_GROUP_W = MIX_HALF // len(POOL_WINDOWS)
NA_ROWS = 8
NA_COLS = 16
ODD_IN = 4 * MIX_HALF

LANES = 128
MXU_DIM = 256
VMEM_LIMIT = 56 * 1024 * 1024

LOG2E = 1.4426950408889634
Q_SCALE = HEAD_DIM ** -0.5 * LOG2E
QK_AHEAD = 2
TOK_TILE = 512
FFN_TILE = 1024
FFN_SUB = 512
EVEN_TQ = 512
NA_QROWS = 8
NA_KROWS = 16
NA_KCOLS = 2 * NA_COLS
NA_WROWS = 4
IN_TILE = 1024
IN_SUB = 512
NA_KSTART = tuple(int(v) for v in np.clip(np.arange(GRID_W // NA_COLS) * NA_COLS - NA_COLS // 2,
                                         0, GRID_W - NA_KCOLS))
NA_COLVAR = (0, 1, 1, 2)
FFN_CHUNKS = (512, 512, 512, 512, 512, 256)
assert sum(FFN_CHUNKS) == D_FF

A_HEAD_PERM = (0, 4, 1, 5, 2, 6, 3, 7)


def _params(*sem):
    return pltpu.CompilerParams(dimension_semantics=sem, vmem_limit_bytes=VMEM_LIMIT)


def _resident(shape, index_map):
    return pl.BlockSpec(shape, index_map, pipeline_mode=pl.Buffered(1))


def _norm_mod(h, g, shift, scale):
    y = h * lax.rsqrt(jnp.mean(h * h, axis=-1, keepdims=True) + EPS)
    return (y * g) * (1.0 + scale) + shift


def _dot(a, b):
    return jnp.dot(a, b, preferred_element_type=F32)


def _dot_nt(a, b):
    return lax.dot_general(a, b, (((1,), (1,)), ((), ())), preferred_element_type=F32)


def _head_rms(x, gain, ones_bd):
    w = x.shape[1]
    step = min(w, MXU_DIM)
    ones = ones_bd[:step, :step]
    parts = []
    for c in range(0, w, step):
        xs = x[:, c:c + step]
        ss = _dot((xs * xs).astype(BF16), ones)
        parts.append(xs * lax.rsqrt(ss / HEAD_DIM + EPS))
    y = parts[0] if len(parts) == 1 else jnp.concatenate(parts, axis=1)
    return y * gain


def _rope(x, cos, sin_signed):
    lane = lax.broadcasted_iota(jnp.int32, (1, LANES), 1)
    first = (lane & 16) == 0
    parts = []
    for c in range(0, x.shape[1], LANES):
        xb = x[:, c:c + LANES]
        partner = jnp.where(first, pltpu.roll(xb, LANES - 16, 1), pltpu.roll(xb, 16, 1))
        parts.append(xb * cos + partner * sin_signed)
    return parts[0] if len(parts) == 1 else jnp.concatenate(parts, axis=1)


def _softmax_pv(s, vcat, sink=None, sum_on_mxu=False):
    m = jnp.max(s, axis=-1, keepdims=True)
    if sink is not None:
        m = jnp.maximum(m, sink)
    p = jnp.exp2(s - m)
    if sum_on_mxu:
        o = _dot(p.astype(BF16), jnp.concatenate([vcat, jnp.ones(vcat.shape, BF16)], axis=1))
        num, den = o[:, :LANES], o[:, LANES:]
    else:
        den = jnp.sum(p, axis=-1, keepdims=True)
        num = _dot(p.astype(BF16), vcat)
    if sink is not None:
        den = den + jnp.exp2(sink - m)
    return num / den


def _pipelined(n, ahead, stage_a, stage_b):
    pending = [stage_a(i) for i in range(min(ahead, n))]
    for i in range(n):
        if i + ahead < n:
            pending.append(stage_a(i + ahead))
        stage_b(i, pending.pop(0))


def _mod_kernel(c_ref, w_ref, b_ref, o_ref):
    sc = jax.nn.silu(c_ref[...])
    o_ref[...] = _dot(sc, w_ref[...]) + b_ref[...]


def _modulation(cvec, ada_w, ada_b):
    depth, d, n = ada_w.shape
    r = cvec.shape[0]
    tn = 3072
    out = pl.pallas_call(
        _mod_kernel,
        grid=(depth, n // tn),
        in_specs=[
            pl.BlockSpec((r, d), lambda l, j: (0, 0)),
            pl.BlockSpec((None, d, tn), lambda l, j: (l, 0, j)),
            pl.BlockSpec((None, 1, tn), lambda l, j: (l, 0, j)),
        ],
        out_specs=pl.BlockSpec((None, r, tn), lambda l, j: (l, 0, j)),
        out_shape=jax.ShapeDtypeStruct((depth, r, n), F32),
        compiler_params=_params("parallel", "parallel"),
        name="adaln_mod",
    )(cvec, ada_w, ada_b.reshape(depth, 1, n))
    return out.reshape(depth, r, N_MOD, d)


def _ffn_kernel(h_ref, mod_ref, g_ref, wgu_ref, wd_ref, o_ref, z_ref, acc_ref, *, mrow):
    sub = z_ref.shape[1]
    nsub = h_ref.shape[0] // sub
    slot = lambda t: lax.rem(pl.program_id(1) + t, 2)

    def norm(t):
        h = h_ref[t * sub:(t + 1) * sub, :]
        z_ref[slot(t)] = _norm_mod(h, g_ref[...], mod_ref[mrow:mrow + 1, :],
                                   mod_ref[mrow + 1:mrow + 2, :]).astype(BF16)

    def residual(t):
        rows = slice(t * sub, (t + 1) * sub)
        o_ref[rows, :] = h_ref[rows, :] + (0.5 * mod_ref[mrow + 2:mrow + 3, :]) * acc_ref[slot(t)]

    norm(0)
    for t in range(nsub):
        acc = None
        off = 0
        for ci, ch in enumerate(FFN_CHUNKS):
            z = z_ref[slot(t)]
            gt = _dot(z, wgu_ref[:, off:off + ch])
            up = _dot(z, wgu_ref[:, D_FF + off:D_FF + off + ch])
            a = (jax.nn.silu(gt) * up).astype(BF16)
            part = _dot(a, wd_ref[off:off + ch, :])
            acc = part if acc is None else acc + part
            off += ch
            if ci == 0 and t >= 1:
                residual(t - 1)
            if ci == 1 and t + 1 < nsub:
                norm(t + 1)
        acc_ref[slot(t)] = acc
    residual(nsub - 1)


def _half_ffn(h, mod, mod_row_of_batch, g, wgu, wd, li, half):
    b, s, d = h.shape
    tm = min(FFN_TILE, s)
    mrow = 6 * half
    return pl.pallas_call(
        functools.partial(_ffn_kernel, mrow=mrow),
        grid=(b, s // tm),
        in_specs=[
            pl.BlockSpec((None, tm, d), lambda bi, i: (bi, i, 0)),
            pl.BlockSpec((None, N_MOD, d), lambda bi, i: (mod_row_of_batch(bi), 0, 0)),
            _resident((1, d), lambda bi, i: (0, 0)),
            _resident((None, None, d, 2 * D_FF), lambda bi, i: (li, half, 0, 0)),
            _resident((None, None, D_FF, d), lambda bi, i: (li, half, 0, 0)),
        ],
        out_specs=pl.BlockSpec((None, tm, d), lambda bi, i: (bi, i, 0)),
        out_shape=jax.ShapeDtypeStruct((b, s, d), F32),
        scratch_shapes=[pltpu.VMEM((2, min(FFN_SUB, tm), d), BF16), pltpu.VMEM((2, min(FFN_SUB, tm), d), F32)],
        compiler_params=_params("parallel", "parallel"),
        name="half_ffn",
    )(h, mod, g.reshape(1, d), wgu, wd)


def _even_inproj_kernel(h_ref, mod_ref, g_ref, win_ref, qg_ref, kg_ref, vg_ref, cos_ref, sin_ref,
                        ones_ref, q_ref, k_ref, v_ref, u_ref, vv_ref, z_ref):
    sub = z_ref.shape[1]
    nsub = h_ref.shape[0] // sub
    ones = ones_ref[...]
    cq, ck, cv, cu, cb = 0, MIX_HALF, MIX_HALF + A_KV, MIX_HALF + 2 * A_KV, 2 * MIX_HALF + 2 * A_KV
    slot = lambda t: lax.rem(pl.program_id(1) + t, 2)

    def norm(t):
        z_ref[slot(t)] = _norm_mod(h_ref[t * sub:(t + 1) * sub, :], g_ref[...], mod_ref[3:4, :],
                                   mod_ref[4:5, :]).astype(BF16)

    norm(0)
    for t in range(nsub):
        rows = slice(t * sub, (t + 1) * sub)
        z = z_ref[slot(t)]
        cos = cos_ref[rows, :]
        sin = sin_ref[rows, :]
        q = _dot(z, win_ref[:, cq:cq + MIX_HALF])
        k = _dot(z, win_ref[:, ck:ck + A_KV])
        if t + 1 < nsub:
            norm(t + 1)
        q = _rope(_head_rms(q, qg_ref[...], ones), cos, sin)
        q_ref[rows, :] = (q * Q_SCALE).astype(BF16)
        bv = jax.nn.gelu(_dot(z, win_ref[:, cb:cb + MIX_HALF]))
        k_ref[rows, :] = _rope(_head_rms(k, kg_ref[...], ones), cos, sin).astype(BF16)
        v_ref[rows, :] = _dot(z, win_ref[:, cv:cv + A_KV]).astype(BF16)
        u = _dot(z, win_ref[:, cu:cu + MIX_HALF])
        vv_ref[rows, :] = _head_rms(bv, vg_ref[...], ones).astype(BF16)
        u_ref[rows, :] = jax.nn.gelu(u)


def _even_inproj(h, mod, mod_row_of_batch, g, win, qg, kg, vg, cos, sin, ones):
    b, s, d = h.shape
    tm = min(IN_TILE, s)
    tok = lambda w: pl.BlockSpec((None, tm, w), lambda bi, i: (bi, i, 0))
    const = lambda shape: _resident(shape, lambda bi, i: (0,) * len(shape))
    return pl.pallas_call(
        _even_inproj_kernel,
        grid=(b, s // tm),
        in_specs=[
            tok(d),
            pl.BlockSpec((None, N_MOD, d), lambda bi, i: (mod_row_of_batch(bi), 0, 0)),
            const((1, d)), const((d, EVEN_IN)), const((1, MIX_HALF)), const((1, A_KV)),
            const((1, MIX_HALF)),
            pl.BlockSpec((tm, LANES), lambda bi, i: (i, 0)),
            pl.BlockSpec((tm, LANES), lambda bi, i: (i, 0)),
            const((MXU_DIM, MXU_DIM)),
        ],
        out_specs=[tok(MIX_HALF), tok(A_KV), tok(A_KV), tok(MIX_HALF), tok(MIX_HALF)],
        out_shape=[
            jax.ShapeDtypeStruct((b, s, MIX_HALF), BF16),
            jax.ShapeDtypeStruct((b, s, A_KV), BF16),
            jax.ShapeDtypeStruct((b, s, A_KV), BF16),
            jax.ShapeDtypeStruct((b, s, MIX_HALF), F32),
            jax.ShapeDtypeStruct((b, s, MIX_HALF), BF16),
        ],
        scratch_shapes=[pltpu.VMEM((2, min(IN_SUB, tm), d), BF16)],
        compiler_params=_params("parallel", "arbitrary"),
        name="even_inproj",
    )(h, mod, g.reshape(1, d), win, qg, kg, vg, cos, sin, ones)


def _even_mixer_kernel(*refs, local, seq):
    if local:
        (q_ref, k_ref, v_ref, kc_ref, vc_ref, sink_ref, u_ref, vv_ref, ws_ref, gb_ref, wout_ref,
         mod_ref, h_ref, o_ref, mix_ref) = refs
    else:
        (q_ref, kc_ref, vc_ref, sink_ref, u_ref, vv_ref, ws_ref, gb_ref, wout_ref,
         mod_ref, h_ref, o_ref, mix_ref) = refs
    tq = q_ref.shape[0]
    nblk = tq // A_BLOCK
    nloc = 3 * A_BLOCK
    lane = lax.broadcasted_iota(jnp.int32, (1, LANES), 1)
    low = lane < HEAD_DIM
    kc = kc_ref[...]
    vc = vc_ref[...]
    groups = N_HEADS // A_KV_HEADS

    def scores(blk):
        r0 = blk * A_BLOCK
        vvb = vv_ref[r0:r0 + B_CHUNK, :]
        for p in range(MIX_HALF // LANES):
            cols = slice(p * LANES, (p + 1) * LANES)
            vsl = vvb[:, cols]
            mixed = jnp.where(low, _dot(ws_ref[2 * p], vsl), _dot(ws_ref[2 * p + 1], vsl)) + gb_ref[:, cols]
            mix_ref[r0:r0 + B_CHUNK, MIX_HALF + p * LANES:MIX_HALF + (p + 1) * LANES] = (
                u_ref[r0:r0 + B_CHUNK, cols] * mixed).astype(BF16)
        qb = q_ref[r0:r0 + A_BLOCK, :]
        q8 = jnp.concatenate(
            [jnp.where(low if j == 0 else jnp.logical_not(low), qb[:, p * LANES:(p + 1) * LANES],
                       jnp.zeros((), BF16))
             for j in range(A_KV_HEADS) for p in range(groups)], axis=0)
        if not local:
            return _dot_nt(q8, kc), vc
        n = pl.program_id(1) * nblk + blk
        start = pl.multiple_of(jnp.clip((n - 1) * A_BLOCK, 0, seq - nloc), A_BLOCK)
        kcat = jnp.concatenate([k_ref[pl.ds(start, nloc), :], kc], axis=0)
        vcat = jnp.concatenate([v_ref[pl.ds(start, nloc), :], vc], axis=0)
        qpos = n * A_BLOCK + lax.broadcasted_iota(jnp.int32, (A_BLOCK, 1), 0)
        kpos = start + lax.broadcasted_iota(jnp.int32, (1, nloc), 1)
        mask = jnp.where(jnp.abs(qpos - kpos) <= WINDOW, 0.0, NEG)
        s = _dot_nt(q8, kcat)
        s_loc = (s[:, :nloc].reshape(N_HEADS, A_BLOCK, nloc) + mask[None]).reshape(N_HEADS * A_BLOCK, nloc)
        return jnp.concatenate([s_loc, s[:, nloc:]], axis=1), vcat

    def values(blk, sv):
        r0 = blk * A_BLOCK
        o = _softmax_pv(sv[0], sv[1], sink_ref[...], sum_on_mxu=not local)
        for p in range(groups):
            lo_rows = o[p * A_BLOCK:(p + 1) * A_BLOCK]
            hi_rows = o[(groups + p) * A_BLOCK:(groups + p + 1) * A_BLOCK]
            mix_ref[r0:r0 + A_BLOCK, p * LANES:(p + 1) * LANES] = jnp.where(low, lo_rows, hi_rows).astype(BF16)

    _pipelined(nblk, QK_AHEAD, scores, values)
    y = _dot(mix_ref[...], wout_ref[...])
    o_ref[...] = h_ref[...] + mod_ref[5:6, :] * y


def _even_mixer(q, k, v, kc, vc, sink_cols, u, vv, ws, gb, wout, mod, mod_row_of_batch, h, local):
    b, s, d = h.shape
    lc = kc.shape[1]
    tq = min(EVEN_TQ, s)
    tok = lambda w: pl.BlockSpec((None, tq, w), lambda bi, i: (bi, i, 0))
    per_batch = lambda rows, w: pl.BlockSpec((None, rows, w), lambda bi, i: (bi, 0, 0))
    const = lambda shape: _resident(shape, lambda bi, i: (0,) * len(shape))
    in_specs = [tok(MIX_HALF)]
    args = [q]
    if local:
        in_specs += [per_batch(s, A_KV), per_batch(s, A_KV)]
        args += [k, v]
    in_specs += [
        per_batch(lc, A_KV), per_batch(lc, A_KV),
        const(sink_cols.shape),
        tok(MIX_HALF), tok(MIX_HALF),
        const(ws.shape), const(gb.shape), const((d, d)),
        pl.BlockSpec((None, N_MOD, d), lambda bi, i: (mod_row_of_batch(bi), 0, 0)),
        tok(d),
    ]
    args += [kc, vc, sink_cols, u, vv, ws, gb, wout, mod, h]
    return pl.pallas_call(
        functools.partial(_even_mixer_kernel, local=local, seq=s),
        grid=(b, s // tq),
        in_specs=in_specs,
        out_specs=tok(d),
        out_shape=jax.ShapeDtypeStruct((b, s, d), F32),
        scratch_shapes=[pltpu.VMEM((tq, d), BF16)],
        compiler_params=_params("parallel", "parallel"),
        name="even_mixer" if local else "even_mixer_ctx",
    )(*args)


def _store_key_windows(dst_ref, dst0, x):
    nmb = len(NA_KSTART)
    for r in range(x.shape[0] // GRID_W):
        jb, r4 = divmod(r, NA_WROWS)
        for m in range(nmb):
            d0 = dst0 + (jb * nmb * NA_WROWS + m * NA_WROWS + r4) * NA_KCOLS
            s0 = r * GRID_W + NA_KSTART[m]
            dst_ref[d0:d0 + NA_KCOLS, :] = x[s0:s0 + NA_KCOLS, :].astype(BF16)


def _odd_inproj_kernel(h_ref, mod_ref, g_ref, win_ref, qg_ref, kg_ref, ones_ref, *refs, kv_only):
    out_refs, z_ref = refs[:-1], refs[-1]
    sub = z_ref.shape[1]
    nsub = h_ref.shape[0] // sub
    ones = ones_ref[...]
    w = MIX_HALF
    slot = lambda t: lax.rem(pl.program_id(1) + t, 2)

    def norm(t):
        z_ref[slot(t)] = _norm_mod(h_ref[t * sub:(t + 1) * sub, :], g_ref[...], mod_ref[3:4, :],
                                   mod_ref[4:5, :]).astype(BF16)

    expand = NA_KCOLS * len(NA_KSTART) // GRID_W
    norm(0)
    for t in range(nsub):
        rows = slice(t * sub, (t + 1) * sub)
        z = z_ref[slot(t)]
        if kv_only:
            k_ref, v_ref = out_refs
            k = _dot(z, win_ref[:, 2 * w:3 * w])
        else:
            xp_ref, q_ref, kw_ref, vw_ref = out_refs
            q = _dot(z, win_ref[:, w:2 * w])
            k = _dot(z, win_ref[:, 2 * w:3 * w])
        if t + 1 < nsub:
            norm(t + 1)
        if not kv_only:
            q_ref[rows, :] = (_head_rms(q, qg_ref[...], ones) * Q_SCALE).astype(BF16)
            xp_ref[rows, :] = _dot(z, win_ref[:, 0:w])
        v = _dot(z, win_ref[:, 3 * w:4 * w])
        k = _head_rms(k, kg_ref[...], ones)
        if kv_only:
            k_ref[rows, :] = k.astype(BF16)
            v_ref[rows, :] = v.astype(BF16)
        else:
            _store_key_windows(kw_ref, t * sub * expand, k)
            _store_key_windows(vw_ref, t * sub * expand, v)


def _odd_inproj(h, mod, mod_row_of_batch, g, win, qg, kg, ones, kv_only):
    b, s, d = h.shape
    tm = min(IN_TILE, s)
    sub = min(IN_SUB, tm)
    expand = NA_KCOLS * len(NA_KSTART) // GRID_W
    tok = lambda w, e=1: pl.BlockSpec((None, e * tm, w), lambda bi, i: (bi, i, 0))
    const = lambda shape: _resident(shape, lambda bi, i: (0,) * len(shape))
    bf = lambda e=1: jax.ShapeDtypeStruct((b, e * s, MIX_HALF), BF16)
    if kv_only:
        out_specs, out_shape = [tok(MIX_HALF)] * 2, [bf(), bf()]
    else:
        assert sub % (NA_WROWS * GRID_W) == 0
        out_specs = [tok(MIX_HALF), tok(MIX_HALF), tok(MIX_HALF, expand), tok(MIX_HALF, expand)]
        out_shape = [jax.ShapeDtypeStruct((b, s, MIX_HALF), F32), bf(), bf(expand), bf(expand)]
    return pl.pallas_call(
        functools.partial(_odd_inproj_kernel, kv_only=kv_only),
        grid=(b, s // tm),
        in_specs=[
            tok(d),
            pl.BlockSpec((None, N_MOD, d), lambda bi, i: (mod_row_of_batch(bi), 0, 0)),
            const((1, d)), const((d, ODD_IN)), const((1, MIX_HALF)), const((1, MIX_HALF)),
            const((MXU_DIM, MXU_DIM)),
        ],
        out_specs=out_specs,
        out_shape=out_shape,
        scratch_shapes=[pltpu.VMEM((2, sub, d), BF16)],
        compiler_params=_params("parallel", "arbitrary"),
        name="odd_inproj_kv" if kv_only else "odd_inproj",
    )(h, mod, g.reshape(1, d), win, qg, kg, ones)


def _na_tables(rows):
    nrb = rows // NA_QROWS
    nc = 2 * NA_COLS - 1
    ridx = np.zeros((3, NA_QROWS, NA_KROWS), np.int32)
    for rv, a in enumerate((0, nrb // 2, nrb - 1)):
        ks = int(np.clip(NA_QROWS * a - NA_ROWS // 2, 0, rows - NA_KROWS))
        r = NA_QROWS * a + np.arange(NA_QROWS)[:, None]
        kr = ks + np.arange(NA_KROWS)[None, :]
        r0 = np.clip(r - NA_ROWS // 2, 0, rows - NA_ROWS)
        ridx[rv] = np.where((kr >= r0) & (kr < r0 + NA_ROWS), kr - r + NA_ROWS - 1, -1)
    csel = np.zeros((3, NA_COLS, NA_KCOLS, nc), np.float32)
    cvalid = np.zeros((3, NA_COLS, NA_KCOLS), bool)
    for cv, m in enumerate((0, 1, 3)):
        qc = NA_COLS * m + np.arange(NA_COLS)[:, None]
        kcol = NA_KSTART[m] + np.arange(NA_KCOLS)[None, :]
        qstart = np.clip(qc - NA_COLS // 2, 0, GRID_W - NA_COLS)
        cvalid[cv] = (kcol >= qstart) & (kcol < qstart + NA_COLS)
        ci = np.clip(kcol - qc, 1 - NA_COLS, NA_COLS - 1) + NA_COLS - 1
        csel[cv] = np.arange(nc) == ci[..., None]
    return ridx, csel, cvalid


def _na_bias_table(rpb, rows):
    ridx, csel, cvalid = _na_tables(rows)
    cb = jnp.einsum("hab,cqjb->chaqj", rpb, csel, precision=lax.Precision.HIGHEST)
    cb = jnp.where(cvalid[:, None, None], cb * LOG2E, NEG)
    outside = jnp.full(cb.shape[:2] + cb.shape[3:], NEG, F32)
    variants = []
    for rv in range(ridx.shape[0]):
        strips = []
        for rl in range(NA_QROWS):
            blocks = [cb[:, :, int(ridx[rv, rl, kl])] if ridx[rv, rl, kl] >= 0 else outside
                      for kl in range(NA_KROWS)]
            strips.append(jnp.concatenate(blocks, axis=-1))
        variants.append(jnp.stack(strips, axis=2))
    tbl = jnp.stack(variants, axis=0)
    return tbl.reshape(3, 3, rpb.shape[0], NA_QROWS * NA_COLS, NA_KROWS * NA_KCOLS)


def _odd_mixer_kernel(xp_ref, xprev_ref, xnext_ref, q_ref, *refs, seq):
    nwb = NA_KROWS // NA_WROWS
    kw_refs, vw_refs = refs[:nwb], refs[nwb:2 * nwb]
    (kc_ref, vc_ref, bias_ref, wpool_ref, cs_ref, wout_ref, mod_ref, h_ref, o_ref, mix_ref,
     xw_ref, ta_ref, tb_ref) = refs[2 * nwb:]
    a = pl.program_id(1)
    last = pl.num_programs(1) - 1
    tq = q_ref.shape[0]
    halo = xprev_ref.shape[0]
    lane = lax.broadcasted_iota(jnp.int32, (1, LANES), 1)
    low = lane < HEAD_DIM

    xw_ref[0:halo, :] = jnp.where(a == 0, 0.0, xprev_ref[...])
    xw_ref[halo:halo + tq, :] = xp_ref[...]
    xw_ref[halo + tq:halo + tq + halo, :] = jnp.where(a == last, 0.0, xnext_ref[...])
    n = tq + 2 * halo
    xw_ref[n:n + halo, :] = jnp.zeros((halo, MIX_HALF), F32)
    ta_ref[n:n + halo, :] = jnp.zeros((halo, C_GROUP_W), F32)
    tb_ref[n:n + halo, :] = jnp.zeros((halo, C_GROUP_W), F32)
    t = a * tq + lax.broadcasted_iota(jnp.int32, (tq, 1), 0)

    def window_sum(cols, w):
        lo = halo - w // 2
        if w < 8:
            acc = None
            for dlt in range(w):
                term = xw_ref[lo + dlt:lo + dlt + tq, cols]
                acc = term if acc is None else acc + term
            return acc
        assert w in (8, 16) and halo == 8
        ta_ref[0:n, :] = xw_ref[0:n, cols] + xw_ref[1:n + 1, cols]
        tb_ref[0:n, :] = ta_ref[0:n, :] + ta_ref[2:n + 2, :]
        ta_ref[0:n, :] = tb_ref[0:n, :] + tb_ref[4:n + 4, :]
        if w == 8:
            return ta_ref[lo:lo + tq, :]
        return ta_ref[0:tq, :] + ta_ref[8:8 + tq, :]

    def pool(gi):
        w = POOL_WINDOWS[gi]
        cols = slice(gi * C_GROUP_W, (gi + 1) * C_GROUP_W)
        acc = window_sum(cols, w)
        cnt =(jnp.minimum(t + (w - w // 2), seq) - jnp.maximum(t - w // 2, 0)).astype(F32)
        pooled = acc / cnt - xp_ref[:, cols]
        y = _dot(pooled.astype(BF16), wpool_ref[gi]) * cs_ref[:, cols]
        mix_ref[:, cols] = y.astype(BF16)

    nmb = GRID_W // NA_COLS
    wrows = NA_WROWS * NA_KCOLS
    nloc = NA_KROWS * NA_KCOLS

    def scores(i):
        p, m = divmod(i, nmb)
        cols = slice(p * LANES, (p + 1) * LANES)
        win = slice(m * wrows, (m + 1) * wrows)
        kcat = jnp.concatenate([r[win, cols] for r in kw_refs] + [kc_ref[:, cols]], axis=0)
        vcat = jnp.concatenate([r[win, cols] for r in vw_refs] + [vc_ref[:, cols]], axis=0)
        qm = jnp.concatenate(
            [q_ref[r * GRID_W + m * NA_COLS:r * GRID_W + (m + 1) * NA_COLS, cols]
             for r in range(NA_QROWS)], axis=0)
        zero = jnp.zeros((), BF16)
        q2 = jnp.concatenate([jnp.where(low, qm, zero), jnp.where(low, zero, qm)], axis=0)
        s = _dot_nt(q2, kcat)
        bias = jnp.concatenate([bias_ref[NA_COLVAR[m], 2 * p], bias_ref[NA_COLVAR[m], 2 * p + 1]], axis=0)
        return jnp.concatenate([s[:, :nloc] + bias, s[:, nloc:]], axis=1), vcat

    def values(i, sv):
        p, m = divmod(i, nmb)
        nq = NA_QROWS * NA_COLS
        o = _softmax_pv(sv[0], sv[1])
        o = jnp.where(low, o[:nq], o[nq:]).astype(BF16)
        for r in range(NA_QROWS):
            mix_ref[r * GRID_W + m * NA_COLS:r * GRID_W + (m + 1) * NA_COLS,
                    MIX_HALF + p * LANES:MIX_HALF + (p + 1) * LANES] = o[r * NA_COLS:(r + 1) * NA_COLS]

    for gi in range(len(POOL_WINDOWS)):
        pool(gi)
    _pipelined((MIX_HALF // LANES) * nmb, QK_AHEAD, scores, values)
    y = _dot(mix_ref[...], wout_ref[...])
    o_ref[...] = h_ref[...] + mod_ref[5:6, :] * y


def _odd_mixer(xp, q, kw, vw, kc, vc, bias_tbl, wpool, cscale, wout, mod, h):
    b, s, d = h.shape
    lc = kc.shape[1]
    tq = NA_QROWS * GRID_W
    nrb = s // tq
    halo = 8
    hb = tq // halo
    nwb = NA_KROWS // NA_WROWS
    wblock = NA_WROWS * NA_KCOLS * len(NA_KSTART)
    tok = lambda w: pl.BlockSpec((None, tq, w), lambda bi, i: (bi, i, 0))
    per_batch = lambda rows, w: _resident((None, rows, w), lambda bi, i: (bi, 0, 0))
    const = lambda shape: _resident(shape, lambda bi, i: (0,) * len(shape))

    def row_variant(i):
        return jnp.where(i == 0, 0, jnp.where(i == nrb - 1, 2, 1))

    def window(j):
        first = lambda i: jnp.clip((NA_QROWS * i - NA_ROWS // 2) // NA_WROWS, 0,
                                   (s // GRID_W - NA_KROWS) // NA_WROWS)
        return pl.BlockSpec((None, wblock, MIX_HALF), lambda bi, i: (bi, first(i) + j, 0))

    return pl.pallas_call(
        functools.partial(_odd_mixer_kernel, seq=s),
        grid=(b, nrb),
        in_specs=[
            tok(MIX_HALF),
            pl.BlockSpec((None, halo, MIX_HALF), lambda bi, i: (bi, jnp.maximum(i * hb - 1, 0), 0)),
            pl.BlockSpec((None, halo, MIX_HALF), lambda bi, i: (bi, jnp.minimum((i + 1) * hb, s // halo - 1), 0)),
            tok(MIX_HALF),
            *[window(j) for j in range(nwb)], *[window(j) for j in range(nwb)],
            per_batch(lc, MIX_HALF), per_batch(lc, MIX_HALF),
            pl.BlockSpec((None,) + bias_tbl.shape[1:], lambda bi, i: (row_variant(i), 0, 0, 0, 0)),
            const(wpool.shape), const(cscale.shape), const((d, d)),
            pl.BlockSpec((None, N_MOD, d), lambda bi, i: (bi, 0, 0)),
            tok(d),
        ],
        out_specs=tok(d),
        out_shape=jax.ShapeDtypeStruct((b, s, d), F32),
        scratch_shapes=[pltpu.VMEM((tq, d), BF16), pltpu.VMEM((tq + 3 * halo, MIX_HALF), F32),
                        pltpu.VMEM((tq + 3 * halo, C_GROUP_W), F32), pltpu.VMEM((tq + 3 * halo, C_GROUP_W), F32)],
        compiler_params=_params("parallel", "arbitrary"),
        name="odd_mixer",
    )(xp, xp, xp, q, *([kw] * nwb), *([vw] * nwb), kc, vc, bias_tbl, wpool, cscale, wout, mod, h)


def _rope_tables(s):
    t = jnp.arange(s)
    row = (t // GRID_W).astype(F32)
    col = (t % GRID_W).astype(F32)
    m = HEAD_DIM // 4
    inv = 1.0 / (ROPE_THETA ** (jnp.arange(m, dtype=F32) / m))
    ang_r = row[:, None] * inv[None, :]
    ang_c = col[:, None] * inv[None, :]
    cos_h = jnp.concatenate([jnp.cos(ang_r)] * 2 + [jnp.cos(ang_c)] * 2, axis=-1)
    sin_h = jnp.concatenate([-jnp.sin(ang_r), jnp.sin(ang_r), -jnp.sin(ang_c), jnp.sin(ang_c)], axis=-1)
    reps = LANES // HEAD_DIM
    return jnp.tile(cos_h, (1, reps)), jnp.tile(sin_h, (1, reps))


def _block_diag_ones():
    idx = np.arange(MXU_DIM) // HEAD_DIM
    return jnp.asarray(idx[:, None] == idx[None, :], dtype=BF16)


def kernel(x, c, ctx, c_ctx, ada_w, ada_b, norm_g, ffn_w_gu, ffn_w_down, ev_w_in, ev_w_out,
           a_q_gain, a_k_gain, a_sink, b_v_gain, b_ws, b_bias, od_w_in, od_w_out, c_w_pool,
           c_scale, d_q_gain, d_k_gain, d_rpb):
    b, s, d = x.shape
    lc = ctx.shape[1]
    depth = ada_w.shape[0]
    assert d == D_MODEL and depth == 2 and s % TOK_TILE == 0 and s // GRID_W >= NA_KROWS
    assert lc % B_CHUNK == 0 and b <= 8

    mod_rows = 16
    cvec = jnp.zeros((mod_rows, d), F32).at[:b].set(c).at[b].set(c_ctx)
    mod = _modulation(cvec, ada_w, ada_b)
    lat_row = lambda bi: bi
    ctx_row = lambda bi: b

    ones = _block_diag_ones()
    cos, sin = _rope_tables(s)
    cos_id = jnp.ones((lc, LANES), F32)
    sin_id = jnp.zeros((lc, LANES), F32)
    tile_heads = lambda gain, n: jnp.tile(gain, n).reshape(1, n * HEAD_DIM)

    wgu = ffn_w_gu.astype(BF16)
    wd = ffn_w_down.astype(BF16)
    h, hc = x, ctx
    li, e = 0, 0
    h = _half_ffn(h, mod[li], lat_row, norm_g[li, 0], wgu, wd, li, 0)
    hc = _half_ffn(hc, mod[li], ctx_row, norm_g[li, 0], wgu, wd, li, 0)
    hsl = lambda hd: slice(hd * HEAD_DIM, (hd + 1) * HEAD_DIM)
    win = jnp.concatenate([ev_w_in[e][:, hsl(hd)] for hd in A_HEAD_PERM] + [ev_w_in[e][:, MIX_HALF:]],
                          axis=1).astype(BF16)
    wout = jnp.concatenate([ev_w_out[e][hsl(hd)] for hd in A_HEAD_PERM] + [ev_w_out[e][MIX_HALF:]],
                           axis=0).astype(BF16)
    qg = tile_heads(a_q_gain[e], N_HEADS)
    kg = tile_heads(a_k_gain[e], A_KV_HEADS)
    vg = b_v_gain[e].reshape(1, MIX_HALF)
    q, k, v, u, vv = _even_inproj(h, mod[li], lat_row, norm_g[li, 1], win, qg, kg, vg, cos, sin, ones)
    qc, kc, vc, uc, vvc = _even_inproj(hc, mod[li], ctx_row, norm_g[li, 1], win, qg, kg, vg,
                                       cos_id, sin_id, ones)
    sink_cols = jnp.repeat(a_sink[e] * LOG2E, A_BLOCK).reshape(N_HEADS * A_BLOCK, 1)
    ws = b_ws[e].astype(BF16)
    gb = jnp.repeat(b_bias[e].T, HEAD_DIM, axis=1)
    h = _even_mixer(q, k, v, kc, vc, sink_cols, u, vv, ws, gb, wout, mod[li], lat_row, h, True)
    hc = _even_mixer(qc, None, None, kc, vc, sink_cols, uc, vvc, ws, gb, wout, mod[li], ctx_row, hc, False)
    h = _half_ffn(h, mod[li], lat_row, norm_g[li, 2], wgu, wd, li, 1)
    hc = _half_ffn(hc, mod[li], ctx_row, norm_g[li, 2], wgu, wd, li, 1)

    li, o = 1, 0
    h = _half_ffn(h, mod[li], lat_row, norm_g[li, 0], wgu, wd, li, 0)
    hc = _half_ffn(hc, mod[li], ctx_row, norm_g[li, 0], wgu, wd, li, 0)
    win = od_w_in[o].astype(BF16)
    wout = od_w_out[o].astype(BF16)
    qg = tile_heads(d_q_gain[o], N_HEADS)
    kg = tile_heads(d_k_gain[o], N_HEADS)
    xp, q, kw, vw = _odd_inproj(h, mod[li], lat_row, norm_g[li, 1], win, qg, kg, ones, False)
    kc, vc = _odd_inproj(hc, mod[li], ctx_row, norm_g[li, 1], win, qg, kg, ones, True)
    bias_tbl = _na_bias_table(d_rpb[o], s // GRID_W)
    h = _odd_mixer(xp, q, kw, vw, kc, vc, bias_tbl, c_w_pool[o].astype(BF16), c_scale[o].reshape(1, MIX_HALF),
                   wout, mod[li], h)
    h = _half_ffn(h, mod[li], lat_row, norm_g[li, 2], wgu, wd, li, 1)
    return h
```

```python
import functools

import numpy as np
import jax
import jax.numpy as jnp
from jax import lax
from jax.experimental import pallas as pl
from jax.experimental.pallas import tpu as pltpu

F32 = jnp.float32
BF16 = jnp.bfloat16

D_MODEL = 1024
GRID_W = 64
HEAD_DIM = 64
MIX_HALF = D_MODEL // 2
N_MOD = 9
D_FF = ((8 * D_MODEL // 3 + 127) // 128) * 128
EPS = 1e-6
ROPE_THETA = 10000.0
NEG = -1e30
N_HEADS = MIX_HALF // HEAD_DIM
A_KV_HEADS = N_HEADS // 4
A_KV = A_KV_HEADS * HEAD_DIM
WINDOW = 128
A_BLOCK = 128
B_CHUNK = 128
EVEN_IN = MIX_HALF + 2 * A_KV + 2 * MIX_HALF
POOL_WINDOWS = (2, 4, 8, 16)
C_GROUP_W = MIX_HALF // len(POOL_WINDOWS)
NA_ROWS = 8
NA_COLS = 16
ODD_IN = 4 * MIX_HALF

LANES = 128
MXU_DIM = 256
VMEM_LIMIT = 56 * 1024 * 1024

LOG2E = 1.4426950408889634
Q_SCALE = HEAD_DIM ** -0.5 * LOG2E
QK_AHEAD = 2
TOK_TILE = 512
FFN_TILE = 1024
FFN_SUB = 512
EVEN_TQ = 512
NA_QROWS = 8
NA_KROWS = 16
NA_KCOLS = 2 * NA_COLS
NA_WROWS = 4
IN_TILE = 1024
IN_SUB = 512
NA_KSTART = tuple(int(v) for v in np.clip(np.arange(GRID_W // NA_COLS) * NA_COLS - NA_COLS // 2,
                                         0, GRID_W - NA_KCOLS))
NA_COLVAR = (0, 1, 1, 2)
FFN_CHUNKS = (512, 512, 512, 512, 512, 256)
assert sum(FFN_CHUNKS) == D_FF

A_HEAD_PERM = (0, 4, 1, 5, 2, 6, 3, 7)


def _params(*sem):
    return pltpu.CompilerParams(dimension_semantics=sem, vmem_limit_bytes=VMEM_LIMIT)


def _resident(shape, index_map):
    return pl.BlockSpec(shape, index_map, pipeline_mode=pl.Buffered(1))


def _norm_mod(h, g, shift, scale):
    y = h * lax.rsqrt(jnp.mean(h * h, axis=-1, keepdims=True) + EPS)
    return (y * g) * (1.0 + scale) + shift


def _dot(a, b):
    return jnp.dot(a, b, preferred_element_type=F32)


def _dot_nt(a, b):
    return lax.dot_general(a, b, (((1,), (1,)), ((), ())), preferred_element_type=F32)


def _head_rms(x, gain, ones_bd):
    w = x.shape[1]
    step = min(w, MXU_DIM)
    ones = ones_bd[:step, :step]
    parts = []
    for c in range(0, w, step):
        xs = x[:, c:c + step]
        ss = _dot((xs * xs).astype(BF16), ones)
        parts.append(xs * lax.rsqrt(ss / HEAD_DIM + EPS))
    y = parts[0] if len(parts) == 1 else jnp.concatenate(parts, axis=1)
    return y * gain


def _rope(x, cos, sin_signed):
    lane = lax.broadcasted_iota(jnp.int32, (1, LANES), 1)
    first = (lane & 16) == 0
    parts = []
    for c in range(0, x.shape[1], LANES):
        xb = x[:, c:c + LANES]
        partner = jnp.where(first, pltpu.roll(xb, LANES - 16, 1), pltpu.roll(xb, 16, 1))
        parts.append(xb * cos + partner * sin_signed)
    return parts[0] if len(parts) == 1 else jnp.concatenate(parts, axis=1)


def _softmax_pv(s, vcat, sink=None, sum_on_mxu=False):
    m = jnp.max(s, axis=-1, keepdims=True)
    if sink is not None:
        m = jnp.maximum(m, sink)
    p = jnp.exp2(s - m)
    if sum_on_mxu:
        o = _dot(p.astype(BF16), jnp.concatenate([vcat, jnp.ones(vcat.shape, BF16)], axis=1))
        num, den = o[:, :LANES], o[:, LANES:]
    else:
        den = jnp.sum(p, axis=-1, keepdims=True)
        num = _dot(p.astype(BF16), vcat)
    if sink is not None:
        den = den + jnp.exp2(sink - m)
    return num / den


def _pipelined(n, ahead, stage_a, stage_b):
    pending = [stage_a(i) for i in range(min(ahead, n))]
    for i in range(n):
        if i + ahead < n:
            pending.append(stage_a(i + ahead))
        stage_b(i, pending.pop(0))


def _mod_kernel(c_ref, w_ref, b_ref, o_ref):
    sc = jax.nn.silu(c_ref[...])
    o_ref[...] = _dot(sc, w_ref[...]) + b_ref[...]


def _modulation(cvec, ada_w, ada_b):
    depth, d, n = ada_w.shape
    r = cvec.shape[0]
    tn = 3072
    out = pl.pallas_call(
        _mod_kernel,
        grid=(depth, n // tn),
        in_specs=[
            pl.BlockSpec((r, d), lambda l, j: (0, 0)),
            pl.BlockSpec((None, d, tn), lambda l, j: (l, 0, j)),
            pl.BlockSpec((None, 1, tn), lambda l, j: (l, 0, j)),
        ],
        out_specs=pl.BlockSpec((None, r, tn), lambda l, j: (l, 0, j)),
        out_shape=jax.ShapeDtypeStruct((depth, r, n), F32),
        compiler_params=_params("parallel", "parallel"),
        name="adaln_mod",
    )(cvec, ada_w, ada_b.reshape(depth, 1, n))
    return out.reshape(depth, r, N_MOD, d)


def _ffn_kernel(h_ref, mod_ref, g_ref, wgu_ref, wd_ref, o_ref, z_ref, acc_ref, *, mrow):
    sub = z_ref.shape[1]
    nsub = h_ref.shape[0] // sub
    slot = lambda t: lax.rem(pl.program_id(1) + t, 2)

    def norm(t):
        h = h_ref[t * sub:(t + 1) * sub, :]
        z_ref[slot(t)] = _norm_mod(h, g_ref[...], mod_ref[mrow:mrow + 1, :],
                                   mod_ref[mrow + 1:mrow + 2, :]).astype(BF16)

    def residual(t):
        rows = slice(t * sub, (t + 1) * sub)
        o_ref[rows, :] = h_ref[rows, :] + (0.5 * mod_ref[mrow + 2:mrow + 3, :]) * acc_ref[slot(t)]

    norm(0)
    for t in range(nsub):
        acc = None
        off = 0
        for ci, ch in enumerate(FFN_CHUNKS):
            z = z_ref[slot(t)]
            gt = _dot(z, wgu_ref[:, off:off + ch])
            up = _dot(z, wgu_ref[:, D_FF + off:D_FF + off + ch])
            a = (jax.nn.silu(gt) * up).astype(BF16)
            part = _dot(a, wd_ref[off:off + ch, :])
            acc = part if acc is None else acc + part
            off += ch
            if ci == 0 and t >= 1:
                residual(t - 1)
            if ci == 1 and t + 1 < nsub:
                norm(t + 1)
        acc_ref[slot(t)] = acc
    residual(nsub - 1)


def _half_ffn(h, mod, mod_row_of_batch, g, wgu, wd, li, half):
    b, s, d = h.shape
    tm = min(FFN_TILE, s)
    mrow = 6 * half
    return pl.pallas_call(
        functools.partial(_ffn_kernel, mrow=mrow),
        grid=(b, s // tm),
        in_specs=[
            pl.BlockSpec((None, tm, d), lambda bi, i: (bi, i, 0)),
            pl.BlockSpec((None, N_MOD, d), lambda bi, i: (mod_row_of_batch(bi), 0, 0)),
            _resident((1, d), lambda bi, i: (0, 0)),
            _resident((None, None, d, 2 * D_FF), lambda bi, i: (li, half, 0, 0)),
            _resident((None, None, D_FF, d), lambda bi, i: (li, half, 0, 0)),
        ],
        out_specs=pl.BlockSpec((None, tm, d), lambda bi, i: (bi, i, 0)),
        out_shape=jax.ShapeDtypeStruct((b, s, d), F32),
        scratch_shapes=[pltpu.VMEM((2, min(FFN_SUB, tm), d), BF16), pltpu.VMEM((2, min(FFN_SUB, tm), d), F32)],
        compiler_params=_params("parallel", "parallel"),
        name="half_ffn",
    )(h, mod, g.reshape(1, d), wgu, wd)


def _even_inproj_kernel(h_ref, mod_ref, g_ref, win_ref, qg_ref, kg_ref, vg_ref, cos_ref, sin_ref,
                        ones_ref, q_ref, k_ref, v_ref, u_ref, vv_ref, z_ref):
    sub = z_ref.shape[1]
    nsub = h_ref.shape[0] // sub
    ones = ones_ref[...]
    cq, ck, cv, cu, cb = 0, MIX_HALF, MIX_HALF + A_KV, MIX_HALF + 2 * A_KV, 2 * MIX_HALF + 2 * A_KV
    slot = lambda t: lax.rem(pl.program_id(1) + t, 2)

    def norm(t):
        z_ref[slot(t)] = _norm_mod(h_ref[t * sub:(t + 1) * sub, :], g_ref[...], mod_ref[3:4, :],
                                   mod_ref[4:5, :]).astype(BF16)

    norm(0)
    for t in range(nsub):
        rows = slice(t * sub, (t + 1) * sub)
        z = z_ref[slot(t)]
        cos = cos_ref[rows, :]
        sin = sin_ref[rows, :]
        q = _dot(z, win_ref[:, cq:cq + MIX_HALF])
        k = _dot(z, win_ref[:, ck:ck + A_KV])
        if t + 1 < nsub:
            norm(t + 1)
        q = _rope(_head_rms(q, qg_ref[...], ones), cos, sin)
        q_ref[rows, :] = (q * Q_SCALE).astype(BF16)
        bv = jax.nn.gelu(_dot(z, win_ref[:, cb:cb + MIX_HALF]))
        k_ref[rows, :] = _rope(_head_rms(k, kg_ref[...], ones), cos, sin).astype(BF16)
        v_ref[rows, :] = _dot(z, win_ref[:, cv:cv + A_KV]).astype(BF16)
        u = _dot(z, win_ref[:, cu:cu + MIX_HALF])
        vv_ref[rows, :] = _head_rms(bv, vg_ref[...], ones).astype(BF16)
        u_ref[rows, :] = jax.nn.gelu(u)


def _even_inproj(h, mod, mod_row_of_batch, g, win, qg, kg, vg, cos, sin, ones):
    b, s, d = h.shape
    tm = min(IN_TILE, s)
    tok = lambda w: pl.BlockSpec((None, tm, w), lambda bi, i: (bi, i, 0))
    const = lambda shape: _resident(shape, lambda bi, i: (0,) * len(shape))
    return pl.pallas_call(
        _even_inproj_kernel,
        grid=(b, s // tm),
        in_specs=[
            tok(d),
            pl.BlockSpec((None, N_MOD, d), lambda bi, i: (mod_row_of_batch(bi), 0, 0)),
            const((1, d)), const((d, EVEN_IN)), const((1, MIX_HALF)), const((1, A_KV)),
            const((1, MIX_HALF)),
            pl.BlockSpec((tm, LANES), lambda bi, i: (i, 0)),
            pl.BlockSpec((tm, LANES), lambda bi, i: (i, 0)),
            const((MXU_DIM, MXU_DIM)),
        ],
        out_specs=[tok(MIX_HALF), tok(A_KV), tok(A_KV), tok(MIX_HALF), tok(MIX_HALF)],
        out_shape=[
            jax.ShapeDtypeStruct((b, s, MIX_HALF), BF16),
            jax.ShapeDtypeStruct((b, s, A_KV), BF16),
            jax.ShapeDtypeStruct((b, s, A_KV), BF16),
            jax.ShapeDtypeStruct((b, s, MIX_HALF), F32),
            jax.ShapeDtypeStruct((b, s, MIX_HALF), BF16),
        ],
        scratch_shapes=[pltpu.VMEM((2, min(IN_SUB, tm), d), BF16)],
        compiler_params=_params("parallel", "arbitrary"),
        name="even_inproj",
    )(h, mod, g.reshape(1, d), win, qg, kg, vg, cos, sin, ones)


def _even_mixer_kernel(*refs, local, seq):
    if local:
        (q_ref, k_ref, v_ref, kc_ref, vc_ref, sink_ref, u_ref, vv_ref, ws_ref, gb_ref, wout_ref,
         mod_ref, h_ref, o_ref, mix_ref) = refs
    else:
        (q_ref, kc_ref, vc_ref, sink_ref, u_ref, vv_ref, ws_ref, gb_ref, wout_ref,
         mod_ref, h_ref, o_ref, mix_ref) = refs
    tq = q_ref.shape[0]
    nblk = tq // A_BLOCK
    nloc = 3 * A_BLOCK
    lane = lax.broadcasted_iota(jnp.int32, (1, LANES), 1)
    low = lane < HEAD_DIM
    kc = kc_ref[...]
    vc = vc_ref[...]
    groups = N_HEADS // A_KV_HEADS

    def scores(blk):
        r0 = blk * A_BLOCK
        vvb = vv_ref[r0:r0 + B_CHUNK, :]
        for p in range(MIX_HALF // LANES):
            cols = slice(p * LANES, (p + 1) * LANES)
            vsl = vvb[:, cols]
            mixed = jnp.where(low, _dot(ws_ref[2 * p], vsl), _dot(ws_ref[2 * p + 1], vsl)) + gb_ref[:, cols]
            mix_ref[r0:r0 + B_CHUNK, MIX_HALF + p * LANES:MIX_HALF + (p + 1) * LANES] = (
                u_ref[r0:r0 + B_CHUNK, cols] * mixed).astype(BF16)
        qb = q_ref[r0:r0 + A_BLOCK, :]
        q8 = jnp.concatenate(
            [jnp.where(low if j == 0 else jnp.logical_not(low), qb[:, p * LANES:(p + 1) * LANES],
                       jnp.zeros((), BF16))
             for j in range(A_KV_HEADS) for p in range(groups)], axis=0)
        if not local:
            return _dot_nt(q8, kc), vc
        n = pl.program_id(1) * nblk + blk
        start = pl.multiple_of(jnp.clip((n - 1) * A_BLOCK, 0, seq - nloc), A_BLOCK)
        kcat = jnp.concatenate([k_ref[pl.ds(start, nloc), :], kc], axis=0)
        vcat = jnp.concatenate([v_ref[pl.ds(start, nloc), :], vc], axis=0)
        qpos = n * A_BLOCK + lax.broadcasted_iota(jnp.int32, (A_BLOCK, 1), 0)
        kpos = start + lax.broadcasted_iota(jnp.int32, (1, nloc), 1)
        mask = jnp.where(jnp.abs(qpos - kpos) <= WINDOW, 0.0, NEG)
        s = _dot_nt(q8, kcat)
        s_loc = (s[:, :nloc].reshape(N_HEADS, A_BLOCK, nloc) + mask[None]).reshape(N_HEADS * A_BLOCK, nloc)
        return jnp.concatenate([s_loc, s[:, nloc:]], axis=1), vcat

    def values(blk, sv):
        r0 = blk * A_BLOCK
        o = _softmax_pv(sv[0], sv[1], sink_ref[...], sum_on_mxu=not local)
        for p in range(groups):
            lo_rows = o[p * A_BLOCK:(p + 1) * A_BLOCK]
            hi_rows = o[(groups + p) * A_BLOCK:(groups + p + 1) * A_BLOCK]
            mix_ref[r0:r0 + A_BLOCK, p * LANES:(p + 1) * LANES] = jnp.where(low, lo_rows, hi_rows).astype(BF16)

    _pipelined(nblk, QK_AHEAD, scores, values)
    y = _dot(mix_ref[...], wout_ref[...])
    o_ref[...] = h_ref[...] + mod_ref[5:6, :] * y


def _even_mixer(q, k, v, kc, vc, sink_cols, u, vv, ws, gb, wout, mod, mod_row_of_batch, h, local):
    b, s, d = h.shape
    lc = kc.shape[1]
    tq = min(EVEN_TQ, s)
    tok = lambda w: pl.BlockSpec((None, tq, w), lambda bi, i: (bi, i, 0))
    per_batch = lambda rows, w: pl.BlockSpec((None, rows, w), lambda bi, i: (bi, 0, 0))
    const = lambda shape: _resident(shape, lambda bi, i: (0,) * len(shape))
    in_specs = [tok(MIX_HALF)]
    args = [q]
    if local:
        in_specs += [per_batch(s, A_KV), per_batch(s, A_KV)]
        args += [k, v]
    in_specs += [
        per_batch(lc, A_KV), per_batch(lc, A_KV),
        const(sink_cols.shape),
        tok(MIX_HALF), tok(MIX_HALF),
        const(ws.shape), const(gb.shape), const((d, d)),
        pl.BlockSpec((None, N_MOD, d), lambda bi, i: (mod_row_of_batch(bi), 0, 0)),
        tok(d),
    ]
    args += [kc, vc, sink_cols, u, vv, ws, gb, wout, mod, h]
    return pl.pallas_call(
        functools.partial(_even_mixer_kernel, local=local, seq=s),
        grid=(b, s // tq),
        in_specs=in_specs,
        out_specs=tok(d),
        out_shape=jax.ShapeDtypeStruct((b, s, d), F32),
        scratch_shapes=[pltpu.VMEM((tq, d), BF16)],
        compiler_params=_params("parallel", "parallel"),
        name="even_mixer" if local else "even_mixer_ctx",
    )(*args)


def _store_key_windows(dst_ref, dst0, x):
    nmb = len(NA_KSTART)
    for r in range(x.shape[0] // GRID_W):
        jb, r4 = divmod(r, NA_WROWS)
        for m in range(nmb):
            d0 = dst0 + (jb * nmb * NA_WROWS + m * NA_WROWS + r4) * NA_KCOLS
            s0 = r * GRID_W + NA_KSTART[m]
            dst_ref[d0:d0 + NA_KCOLS, :] = x[s0:s0 + NA_KCOLS, :].astype(BF16)


def _odd_inproj_kernel(h_ref, mod_ref, g_ref, win_ref, qg_ref, kg_ref, ones_ref, *refs, kv_only):
    out_refs, z_ref = refs[:-1], refs[-1]
    sub = z_ref.shape[1]
    nsub = h_ref.shape[0] // sub
    ones = ones_ref[...]
    w = MIX_HALF
    slot = lambda t: lax.rem(pl.program_id(1) + t, 2)

    def norm(t):
        z_ref[slot(t)] = _norm_mod(h_ref[t * sub:(t + 1) * sub, :], g_ref[...], mod_ref[3:4, :],
                                   mod_ref[4:5, :]).astype(BF16)

    expand = NA_KCOLS * len(NA_KSTART) // GRID_W
    norm(0)
    for t in range(nsub):
        rows = slice(t * sub, (t + 1) * sub)
        z = z_ref[slot(t)]
        if kv_only:
            k_ref, v_ref = out_refs
            k = _dot(z, win_ref[:, 2 * w:3 * w])
        else:
            xp_ref, q_ref, kw_ref, vw_ref = out_refs
            q = _dot(z, win_ref[:, w:2 * w])
            k = _dot(z, win_ref[:, 2 * w:3 * w])
        if t + 1 < nsub:
            norm(t + 1)
        if not kv_only:
            q_ref[rows, :] = (_head_rms(q, qg_ref[...], ones) * Q_SCALE).astype(BF16)
            xp_ref[rows, :] = _dot(z, win_ref[:, 0:w])
        v = _dot(z, win_ref[:, 3 * w:4 * w])
        k = _head_rms(k, kg_ref[...], ones)
        if kv_only:
            k_ref[rows, :] = k.astype(BF16)
            v_ref[rows, :] = v.astype(BF16)
        else:
            _store_key_windows(kw_ref, t * sub * expand, k)
            _store_key_windows(vw_ref, t * sub * expand, v)


def _odd_inproj(h, mod, mod_row_of_batch, g, win, qg, kg, ones, kv_only):
    b, s, d = h.shape
    tm = min(IN_TILE, s)
    sub = min(IN_SUB, tm)
    expand = NA_KCOLS * len(NA_KSTART) // GRID_W
    tok = lambda w, e=1: pl.BlockSpec((None, e * tm, w), lambda bi, i: (bi, i, 0))
    const = lambda shape: _resident(shape, lambda bi, i: (0,) * len(shape))
    bf = lambda e=1: jax.ShapeDtypeStruct((b, e * s, MIX_HALF), BF16)
    if kv_only:
        out_specs, out_shape = [tok(MIX_HALF)] * 2, [bf(), bf()]
    else:
        assert sub % (NA_WROWS * GRID_W) == 0
        out_specs = [tok(MIX_HALF), tok(MIX_HALF), tok(MIX_HALF, expand), tok(MIX_HALF, expand)]
        out_shape = [jax.ShapeDtypeStruct((b, s, MIX_HALF), F32), bf(), bf(expand), bf(expand)]
    return pl.pallas_call(
        functools.partial(_odd_inproj_kernel, kv_only=kv_only),
        grid=(b, s // tm),
        in_specs=[
            tok(d),
            pl.BlockSpec((None, N_MOD, d), lambda bi, i: (mod_row_of_batch(bi), 0, 0)),
            const((1, d)), const((d, ODD_IN)), const((1, MIX_HALF)), const((1, MIX_HALF)),
            const((MXU_DIM, MXU_DIM)),
        ],
        out_specs=out_specs,
        out_shape=out_shape,
        scratch_shapes=[pltpu.VMEM((2, sub, d), BF16)],
        compiler_params=_params("parallel", "arbitrary"),
        name="odd_inproj_kv" if kv_only else "odd_inproj",
    )(h, mod, g.reshape(1, d), win, qg, kg, ones)


def _na_tables(rows):
    nrb = rows // NA_QROWS
    nc = 2 * NA_COLS - 1
    ridx = np.zeros((3, NA_QROWS, NA_KROWS), np.int32)
    for rv, a in enumerate((0, nrb // 2, nrb - 1)):
        ks = int(np.clip(NA_QROWS * a - NA_ROWS // 2, 0, rows - NA_KROWS))
        r = NA_QROWS * a + np.arange(NA_QROWS)[:, None]
        kr = ks + np.arange(NA_KROWS)[None, :]
        r0 = np.clip(r - NA_ROWS // 2, 0, rows - NA_ROWS)
        ridx[rv] = np.where((kr >= r0) & (kr < r0 + NA_ROWS), kr - r + NA_ROWS - 1, -1)
    csel = np.zeros((3, NA_COLS, NA_KCOLS, nc), np.float32)
    cvalid = np.zeros((3, NA_COLS, NA_KCOLS), bool)
    for cv, m in enumerate((0, 1, 3)):
        qc = NA_COLS * m + np.arange(NA_COLS)[:, None]
        kcol = NA_KSTART[m] + np.arange(NA_KCOLS)[None, :]
        qstart = np.clip(qc - NA_COLS // 2, 0, GRID_W - NA_COLS)
        cvalid[cv] = (kcol >= qstart) & (kcol < qstart + NA_COLS)
        ci = np.clip(kcol - qc, 1 - NA_COLS, NA_COLS - 1) + NA_COLS - 1
        csel[cv] = np.arange(nc) == ci[..., None]
    return ridx, csel, cvalid


def _na_bias_table(rpb, rows):
    ridx, csel, cvalid = _na_tables(rows)
    nr = 2 * NA_ROWS - 1
    nk = NA_KROWS * NA_KCOLS
    cb = jnp.einsum("hab,cqjb->chqaj", rpb, csel, precision=lax.Precision.HIGHEST)
    cb = jnp.where(cvalid[:, None, :, None, :], cb * LOG2E, NEG)
    flat = jnp.pad(cb.reshape(cb.shape[:3] + (nr * NA_KCOLS,)), ((0, 0),) * 3 + ((nk, nk),))
    inside = np.zeros((ridx.shape[0], NA_QROWS, nk), bool)
    variants = []
    for rv in range(ridx.shape[0]):
        strips = []
        for rl in range(NA_QROWS):
            kl0 = int(np.argmax(ridx[rv, rl] >= 0))
            a0 = int(ridx[rv, rl, kl0])
            assert (ridx[rv, rl, kl0:kl0 + NA_ROWS] == a0 + np.arange(NA_ROWS)).all()
            assert (np.delete(ridx[rv, rl], np.s_[kl0:kl0 + NA_ROWS]) < 0).all()
            inside[rv, rl, kl0 * NA_KCOLS:(kl0 + NA_ROWS) * NA_KCOLS] = True
            off = nk + (a0 - kl0) * NA_KCOLS
            strips.append(flat[..., off:off + nk])
        variants.append(jnp.stack(strips, axis=2))
    tbl = jnp.stack(variants, axis=0)
    tbl = jnp.where(inside[:, None, None, :, None, :], tbl, NEG)
    return tbl.reshape(3, 3, rpb.shape[0], NA_QROWS * NA_COLS, nk)


def _odd_mixer_kernel(xp_ref, xprev_ref, xnext_ref, q_ref, *refs, seq):
    nwb = NA_KROWS // NA_WROWS
    kw_refs, vw_refs = refs[:nwb], refs[nwb:2 * nwb]
    (kc_ref, vc_ref, bias_ref, wpool_ref, cs_ref, wout_ref, mod_ref, h_ref, o_ref, mix_ref,
     xw_ref, ta_ref, tb_ref) = refs[2 * nwb:]
    a = pl.program_id(1)
    last = pl.num_programs(1) - 1
    tq = q_ref.shape[0]
    halo = xprev_ref.shape[0]
    lane = lax.broadcasted_iota(jnp.int32, (1, LANES), 1)
    low = lane < HEAD_DIM

    xw_ref[0:halo, :] = jnp.where(a == 0, 0.0, xprev_ref[...])
    xw_ref[halo:halo + tq, :] = xp_ref[...]
    xw_ref[halo + tq:halo + tq + halo, :] = jnp.where(a == last, 0.0, xnext_ref[...])
    n = tq + 2 * halo
    xw_ref[n:n + halo, :] = jnp.zeros((halo, MIX_HALF), F32)
    ta_ref[n:n + halo, :] = jnp.zeros((halo, C_GROUP_W), F32)
    tb_ref[n:n + halo, :] = jnp.zeros((halo, C_GROUP_W), F32)
    t = a * tq + lax.broadcasted_iota(jnp.int32, (tq, 1), 0)

    def window_sum(cols, w):
        lo = halo - w // 2
        if w < 8:
            acc = None
            for dlt in range(w):
                term = xw_ref[lo + dlt:lo + dlt + tq, cols]
                acc = term if acc is None else acc + term
            return acc
        assert w in (8, 16) and halo == 8
        ta_ref[0:n, :] = xw_ref[0:n, cols] + xw_ref[1:n + 1, cols]
        tb_ref[0:n, :] = ta_ref[0:n, :] + ta_ref[2:n + 2, :]
        ta_ref[0:n, :] = tb_ref[0:n, :] + tb_ref[4:n + 4, :]
        if w == 8:
            return ta_ref[lo:lo + tq, :]
        return ta_ref[0:tq, :] + ta_ref[8:8 + tq, :]

    def pool(gi):
        w = POOL_WINDOWS[gi]
        cols = slice(gi * C_GROUP_W, (gi + 1) * C_GROUP_W)
        acc = window_sum(cols, w)
        cnt =(jnp.minimum(t + (w - w // 2), seq) - jnp.maximum(t - w // 2, 0)).astype(F32)
        pooled = acc / cnt - xp_ref[:, cols]
        y = _dot(pooled.astype(BF16), wpool_ref[gi]) * cs_ref[:, cols]
        mix_ref[:, cols] = y.astype(BF16)

    nmb = GRID_W // NA_COLS
    wrows = NA_WROWS * NA_KCOLS
    nloc = NA_KROWS * NA_KCOLS

    def scores(i):
        p, m = divmod(i, nmb)
        cols = slice(p * LANES, (p + 1) * LANES)
        win = slice(m * wrows, (m + 1) * wrows)
        kcat = jnp.concatenate([r[win, cols] for r in kw_refs] + [kc_ref[:, cols]], axis=0)
        vcat = jnp.concatenate([r[win, cols] for r in vw_refs] + [vc_ref[:, cols]], axis=0)
        qm = jnp.concatenate(
            [q_ref[r * GRID_W + m * NA_COLS:r * GRID_W + (m + 1) * NA_COLS, cols]
             for r in range(NA_QROWS)], axis=0)
        zero = jnp.zeros((), BF16)
        q2 = jnp.concatenate([jnp.where(low, qm, zero), jnp.where(low, zero, qm)], axis=0)
        s = _dot_nt(q2, kcat)
        bias = jnp.concatenate([bias_ref[NA_COLVAR[m], 2 * p], bias_ref[NA_COLVAR[m], 2 * p + 1]], axis=0)
        return jnp.concatenate([s[:, :nloc] + bias, s[:, nloc:]], axis=1), vcat

    def values(i, sv):
        p, m = divmod(i, nmb)
        nq = NA_QROWS * NA_COLS
        o = _softmax_pv(sv[0], sv[1])
        o = jnp.where(low, o[:nq], o[nq:]).astype(BF16)
        for r in range(NA_QROWS):
            mix_ref[r * GRID_W + m * NA_COLS:r * GRID_W + (m + 1) * NA_COLS,
                    MIX_HALF + p * LANES:MIX_HALF + (p + 1) * LANES] = o[r * NA_COLS:(r + 1) * NA_COLS]

    for gi in range(len(POOL_WINDOWS)):
        pool(gi)
    _pipelined((MIX_HALF // LANES) * nmb, QK_AHEAD, scores, values)
    y = _dot(mix_ref[...], wout_ref[...])
    o_ref[...] = h_ref[...] + mod_ref[5:6, :] * y


def _odd_mixer(xp, q, kw, vw, kc, vc, bias_tbl, wpool, cscale, wout, mod, h):
    b, s, d = h.shape
    lc = kc.shape[1]
    tq = NA_QROWS * GRID_W
    nrb = s // tq
    halo = 8
    hb = tq // halo
    nwb = NA_KROWS // NA_WROWS
    wblock = NA_WROWS * NA_KCOLS * len(NA_KSTART)
    tok = lambda w: pl.BlockSpec((None, tq, w), lambda bi, i: (bi, i, 0))
    per_batch = lambda rows, w: _resident((None, rows, w), lambda bi, i: (bi, 0, 0))
    const = lambda shape: _resident(shape, lambda bi, i: (0,) * len(shape))

    def row_variant(i):
        return jnp.where(i == 0, 0, jnp.where(i == nrb - 1, 2, 1))

    def window(j):
        first = lambda i: jnp.clip((NA_QROWS * i - NA_ROWS // 2) // NA_WROWS, 0,
                                   (s // GRID_W - NA_KROWS) // NA_WROWS)
        return pl.BlockSpec((None, wblock, MIX_HALF), lambda bi, i: (bi, first(i) + j, 0))

    return pl.pallas_call(
        functools.partial(_odd_mixer_kernel, seq=s),
        grid=(b, nrb),
        in_specs=[
            tok(MIX_HALF),
            pl.BlockSpec((None, halo, MIX_HALF), lambda bi, i: (bi, jnp.maximum(i * hb - 1, 0), 0)),
            pl.BlockSpec((None, halo, MIX_HALF), lambda bi, i: (bi, jnp.minimum((i + 1) * hb, s // halo - 1), 0)),
            tok(MIX_HALF),
            *[window(j) for j in range(nwb)], *[window(j) for j in range(nwb)],
            per_batch(lc, MIX_HALF), per_batch(lc, MIX_HALF),
            pl.BlockSpec((None,) + bias_tbl.shape[1:], lambda bi, i: (row_variant(i), 0, 0, 0, 0)),
            const(wpool.shape), const(cscale.shape), const((d, d)),
            pl.BlockSpec((None, N_MOD, d), lambda bi, i: (bi, 0, 0)),
            tok(d),
        ],
        out_specs=tok(d),
        out_shape=jax.ShapeDtypeStruct((b, s, d), F32),
        scratch_shapes=[pltpu.VMEM((tq, d), BF16), pltpu.VMEM((tq + 3 * halo, MIX_HALF), F32),
                        pltpu.VMEM((tq + 3 * halo, C_GROUP_W), F32), pltpu.VMEM((tq + 3 * halo, C_GROUP_W), F32)],
        compiler_params=_params("parallel", "arbitrary"),
        name="odd_mixer",
    )(xp, xp, xp, q, *([kw] * nwb), *([vw] * nwb), kc, vc, bias_tbl, wpool, cscale, wout, mod, h)


def _rope_tables(s):
    t = jnp.arange(s)
    row = (t // GRID_W).astype(F32)
    col = (t % GRID_W).astype(F32)
    m = HEAD_DIM // 4
    inv = 1.0 / (ROPE_THETA ** (jnp.arange(m, dtype=F32) / m))
    ang_r = row[:, None] * inv[None, :]
    ang_c = col[:, None] * inv[None, :]
    cos_h = jnp.concatenate([jnp.cos(ang_r)] * 2 + [jnp.cos(ang_c)] * 2, axis=-1)
    sin_h = jnp.concatenate([-jnp.sin(ang_r), jnp.sin(ang_r), -jnp.sin(ang_c), jnp.sin(ang_c)], axis=-1)
    reps = LANES // HEAD_DIM
    return jnp.tile(cos_h, (1, reps)), jnp.tile(sin_h, (1, reps))


def _block_diag_ones():
    idx = np.arange(MXU_DIM) // HEAD_DIM
    return jnp.asarray(idx[:, None] == idx[None, :], dtype=BF16)


def kernel(x, c, ctx, c_ctx, ada_w, ada_b, norm_g, ffn_w_gu, ffn_w_down, ev_w_in, ev_w_out,
           a_q_gain, a_k_gain, a_sink, b_v_gain, b_ws, b_bias, od_w_in, od_w_out, c_w_pool,
           c_scale, d_q_gain, d_k_gain, d_rpb):
    b, s, d = x.shape
    lc = ctx.shape[1]
    depth = ada_w.shape[0]
    assert d == D_MODEL and depth == 2 and s % TOK_TILE == 0 and s // GRID_W >= NA_KROWS
    assert lc % B_CHUNK == 0 and b <= 8

    mod_rows = 16
    cvec = jnp.zeros((mod_rows, d), F32).at[:b].set(c).at[b].set(c_ctx)
    mod = _modulation(cvec, ada_w, ada_b)
    lat_row = lambda bi: bi
    ctx_row = lambda bi: b

    ones = _block_diag_ones()
    cos, sin = _rope_tables(s)
    flat = lambda t: t.reshape(1, b * lc, t.shape[-1])
    unflat = lambda t: t.reshape(b, lc, t.shape[-1])
    cos_id = jnp.ones((b * lc, LANES), F32)
    sin_id = jnp.zeros((b * lc, LANES), F32)
    tile_heads = lambda gain, n: jnp.tile(gain, n).reshape(1, n * HEAD_DIM)

    wgu = ffn_w_gu.astype(BF16)
    wd = ffn_w_down.astype(BF16)
    h, hc = x, flat(ctx)
    li, e = 0, 0
    h = _half_ffn(h, mod[li], lat_row, norm_g[li, 0], wgu, wd, li, 0)
    hc = _half_ffn(hc, mod[li], ctx_row, norm_g[li, 0], wgu, wd, li, 0)
    hsl = lambda hd: slice(hd * HEAD_DIM, (hd + 1) * HEAD_DIM)
    win = jnp.concatenate([ev_w_in[e][:, hsl(hd)] for hd in A_HEAD_PERM] + [ev_w_in[e][:, MIX_HALF:]],
                          axis=1).astype(BF16)
    wout = jnp.concatenate([ev_w_out[e][hsl(hd)] for hd in A_HEAD_PERM] + [ev_w_out[e][MIX_HALF:]],
                           axis=0).astype(BF16)
    qg = tile_heads(a_q_gain[e], N_HEADS)
    kg = tile_heads(a_k_gain[e], A_KV_HEADS)
    vg = b_v_gain[e].reshape(1, MIX_HALF)
    q, k, v, u, vv = _even_inproj(h, mod[li], lat_row, norm_g[li, 1], win, qg, kg, vg, cos, sin, ones)
    qc, kc, vc, uc, vvc = map(unflat, _even_inproj(hc, mod[li], ctx_row, norm_g[li, 1], win, qg, kg, vg,
                                                   cos_id, sin_id, ones))
    sink_cols = jnp.repeat(a_sink[e] * LOG2E, A_BLOCK).reshape(N_HEADS * A_BLOCK, 1)
    ws = b_ws[e].astype(BF16)
    gb = jnp.repeat(b_bias[e].T, HEAD_DIM, axis=1)
    h = _even_mixer(q, k, v, kc, vc, sink_cols, u, vv, ws, gb, wout, mod[li], lat_row, h, True)
    hc = flat(_even_mixer(qc, None, None, kc, vc, sink_cols, uc, vvc, ws, gb, wout, mod[li], ctx_row,
                          unflat(hc), False))
    h = _half_ffn(h, mod[li], lat_row, norm_g[li, 2], wgu, wd, li, 1)
    hc = _half_ffn(hc, mod[li], ctx_row, norm_g[li, 2], wgu, wd, li, 1)

    li, o = 1, 0
    h = _half_ffn(h, mod[li], lat_row, norm_g[li, 0], wgu, wd, li, 0)
    hc = _half_ffn(hc, mod[li], ctx_row, norm_g[li, 0], wgu, wd, li, 0)
    win = od_w_in[o].astype(BF16)
    wout = od_w_out[o].astype(BF16)
    qg = tile_heads(d_q_gain[o], N_HEADS)
    kg = tile_heads(d_k_gain[o], N_HEADS)
    xp, q, kw, vw = _odd_inproj(h, mod[li], lat_row, norm_g[li, 1], win, qg, kg, ones, False)
    kc, vc = map(unflat, _odd_inproj(hc, mod[li], ctx_row, norm_g[li, 1], win, qg, kg, ones, True))
    bias_tbl = _na_bias_table(d_rpb[o], s // GRID_W)
    h = _odd_mixer(xp, q, kw, vw, kc, vc, bias_tbl, c_w_pool[o].astype(BF16), c_scale[o].reshape(1, MIX_HALF),
                   wout, mod[li], h)
    h = _half_ffn(h, mod[li], lat_row, norm_g[li, 2], wgu, wd, li, 1)
    return h
```

```python
import functools

import numpy as np
import jax
import jax.numpy as jnp
from jax import lax
from jax.experimental import pallas as pl
from jax.experimental.pallas import tpu as pltpu

F32 = jnp.float32
BF16 = jnp.bfloat16

D_MODEL = 1024
GRID_W = 64
HEAD_DIM = 64
MIX_HALF = D_MODEL // 2
N_MOD = 9
D_FF = ((8 * D_MODEL // 3 + 127) // 128) * 128
EPS = 1e-6
ROPE_THETA = 10000.0
NEG = -1e30
N_HEADS = MIX_HALF // HEAD_DIM
A_KV_HEADS = N_HEADS // 4
A_KV = A_KV_HEADS * HEAD_DIM
WINDOW = 128
A_BLOCK = 128
B_CHUNK = 128
EVEN_IN = MIX_HALF + 2 * A_KV + 2 * MIX_HALF
POOL_WINDOWS = (2, 4, 8, 16)
C_GROUP_W = MIX_HALF // len(POOL_WINDOWS)
NA_ROWS = 8
NA_COLS = 16
ODD_IN = 4 * MIX_HALF

LANES = 128
MXU_DIM = 256
VMEM_LIMIT = 56 * 1024 * 1024

LOG2E = 1.4426950408889634
Q_SCALE = HEAD_DIM ** -0.5 * LOG2E
QK_AHEAD = 2
TOK_TILE = 512
FFN_TILE = 1024
FFN_SUB = 512
EVEN_TQ = 512
NA_QROWS = 8
NA_KROWS = 16
NA_KCOLS = 2 * NA_COLS
NA_WROWS = 4
IN_TILE = 1024
IN_SUB = 512
NA_KSTART = tuple(int(v) for v in np.clip(np.arange(GRID_W // NA_COLS) * NA_COLS - NA_COLS // 2,
                                         0, GRID_W - NA_KCOLS))
NA_COLVAR = (0, 1, 1, 2)
FFN_CHUNKS = (512, 512, 512, 512, 512, 256)
assert sum(FFN_CHUNKS) == D_FF

A_HEAD_PERM = (0, 4, 1, 5, 2, 6, 3, 7)


def _params(*sem):
    return pltpu.CompilerParams(dimension_semantics=sem, vmem_limit_bytes=VMEM_LIMIT)


def _resident(shape, index_map):
    return pl.BlockSpec(shape, index_map, pipeline_mode=pl.Buffered(1))


def _norm_mod(h, g, shift, scale):
    y = h * lax.rsqrt(jnp.mean(h * h, axis=-1, keepdims=True) + EPS)
    return (y * g) * (1.0 + scale) + shift


def _dot(a, b):
    return jnp.dot(a, b, preferred_element_type=F32)


def _dot_nt(a, b):
    return lax.dot_general(a, b, (((1,), (1,)), ((), ())), preferred_element_type=F32)


def _head_rms(x, gain, ones_bd):
    w = x.shape[1]
    step = min(w, MXU_DIM)
    ones = ones_bd[:step, :step]
    parts = []
    for c in range(0, w, step):
        xs = x[:, c:c + step]
        ss = _dot((xs * xs).astype(BF16), ones)
        parts.append(xs * lax.rsqrt(ss / HEAD_DIM + EPS))
    y = parts[0] if len(parts) == 1 else jnp.concatenate(parts, axis=1)
    return y * gain


def _rope(x, cos, sin_signed):
    lane = lax.broadcasted_iota(jnp.int32, (1, LANES), 1)
    first = (lane & 16) == 0
    parts = []
    for c in range(0, x.shape[1], LANES):
        xb = x[:, c:c + LANES]
        partner = jnp.where(first, pltpu.roll(xb, LANES - 16, 1), pltpu.roll(xb, 16, 1))
        parts.append(xb * cos + partner * sin_signed)
    return parts[0] if len(parts) == 1 else jnp.concatenate(parts, axis=1)


def _softmax_pv(s, vcat, sink=None, sum_on_mxu=False):
    m = jnp.max(s, axis=-1, keepdims=True)
    if sink is not None:
        m = jnp.maximum(m, sink)
    p = jnp.exp2(s - m)
    if sum_on_mxu:
        o = _dot(p.astype(BF16), jnp.concatenate([vcat, jnp.ones(vcat.shape, BF16)], axis=1))
        num, den = o[:, :LANES], o[:, LANES:]
    else:
        den = jnp.sum(p, axis=-1, keepdims=True)
        num = _dot(p.astype(BF16), vcat)
    if sink is not None:
        den = den + jnp.exp2(sink - m)
    return num / den


def _pipelined(n, ahead, stage_a, stage_b):
    pending = [stage_a(i) for i in range(min(ahead, n))]
    for i in range(n):
        if i + ahead < n:
            pending.append(stage_a(i + ahead))
        stage_b(i, pending.pop(0))


def _mod_kernel(c_ref, w_ref, b_ref, o_ref):
    sc = jax.nn.silu(c_ref[...])
    o_ref[...] = _dot(sc, w_ref[...]) + b_ref[...]


def _modulation(cvec, ada_w, ada_b):
    depth, d, n = ada_w.shape
    r = cvec.shape[0]
    tn = 3072
    out = pl.pallas_call(
        _mod_kernel,
        grid=(depth, n // tn),
        in_specs=[
            pl.BlockSpec((r, d), lambda l, j: (0, 0)),
            pl.BlockSpec((None, d, tn), lambda l, j: (l, 0, j)),
            pl.BlockSpec((None, 1, tn), lambda l, j: (l, 0, j)),
        ],
        out_specs=pl.BlockSpec((None, r, tn), lambda l, j: (l, 0, j)),
        out_shape=jax.ShapeDtypeStruct((depth, r, n), F32),
        compiler_params=_params("parallel", "parallel"),
        name="adaln_mod",
    )(cvec, ada_w, ada_b.reshape(depth, 1, n))
    return out.reshape(depth, r, N_MOD, d)


def _ffn_kernel(h_ref, mod_ref, g_ref, wgu_ref, wd_ref, *refs, mrow, cast_next):
    if cast_next:
        src_gu_ref, src_d_ref, o_ref, dst_gu_ref, dst_d_ref, z_ref, acc_ref = refs
        dst_gu_ref[...] = src_gu_ref[...].astype(BF16)
        dst_d_ref[...] = src_d_ref[...].astype(BF16)
    else:
        o_ref, z_ref, acc_ref = refs
    sub = z_ref.shape[1]
    nsub = h_ref.shape[0] // sub
    slot = lambda t: lax.rem(pl.program_id(1) + t, 2)

    def norm(t):
        h = h_ref[t * sub:(t + 1) * sub, :]
        z_ref[slot(t)] = _norm_mod(h, g_ref[...], mod_ref[mrow:mrow + 1, :],
                                   mod_ref[mrow + 1:mrow + 2, :]).astype(BF16)

    def residual(t):
        rows = slice(t * sub, (t + 1) * sub)
        o_ref[rows, :] = h_ref[rows, :] + (0.5 * mod_ref[mrow + 2:mrow + 3, :]) * acc_ref[slot(t)]

    norm(0)
    for t in range(nsub):
        acc = None
        off = 0
        for ci, ch in enumerate(FFN_CHUNKS):
            z = z_ref[slot(t)]
            gt = _dot(z, wgu_ref[:, off:off + ch])
            up = _dot(z, wgu_ref[:, D_FF + off:D_FF + off + ch])
            a = (jax.nn.silu(gt) * up).astype(BF16)
            part = _dot(a, wd_ref[off:off + ch, :])
            acc = part if acc is None else acc + part
            off += ch
            if ci == 0 and t >= 1:
                residual(t - 1)
            if ci == 1 and t + 1 < nsub:
                norm(t + 1)
        acc_ref[slot(t)] = acc
    residual(nsub - 1)


def _half_ffn(h, mod, mod_row_of_batch, g, wgu, wd, half, cast_next=None):
    b, s, d = h.shape
    tm = min(FFN_TILE, s)
    nt = s // tm
    mrow = 6 * half
    in_specs = [
        pl.BlockSpec((None, tm, d), lambda bi, i: (bi, i, 0)),
        pl.BlockSpec((None, N_MOD, d), lambda bi, i: (mod_row_of_batch(bi), 0, 0)),
        _resident((1, d), lambda bi, i: (0, 0)),
        _resident((d, 2 * D_FF), lambda bi, i: (0, 0)),
        _resident((D_FF, d), lambda bi, i: (0, 0)),
    ]
    args = [h, mod, g.reshape(1, d), wgu, wd]
    out_specs = [pl.BlockSpec((None, tm, d), lambda bi, i: (bi, i, 0))]
    out_shape = [jax.ShapeDtypeStruct((b, s, d), F32)]
    if cast_next is not None:
        src_gu, src_d, nl, nh = cast_next
        steps = b * nt
        rows_gu = d // steps
        rep = 1
        while (D_FF * rep // steps) % 16:
            rep *= 2
        rows_d = D_FF * rep // steps
        assert rows_gu * steps == d and rows_gu % 16 == 0 and rows_d * steps == D_FF * rep and rep <= steps
        in_specs += [
            pl.BlockSpec((None, None, rows_gu, 2 * D_FF), lambda bi, i: (nl, nh, bi * nt + i, 0)),
            pl.BlockSpec((None, None, rows_d, d), lambda bi, i: (nl, nh, (bi * nt + i) // rep, 0)),
        ]
        args += [src_gu, src_d]
        out_specs += [pl.BlockSpec((rows_gu, 2 * D_FF), lambda bi, i: (bi * nt + i, 0)),
                      pl.BlockSpec((rows_d, d), lambda bi, i: ((bi * nt + i) // rep, 0))]
        out_shape += [jax.ShapeDtypeStruct((d, 2 * D_FF), BF16), jax.ShapeDtypeStruct((D_FF, d), BF16)]
    outs = pl.pallas_call(
        functools.partial(_ffn_kernel, mrow=mrow, cast_next=cast_next is not None),
        grid=(b, nt),
        in_specs=in_specs,
        out_specs=out_specs,
        out_shape=out_shape,
        scratch_shapes=[pltpu.VMEM((2, min(FFN_SUB, tm), d), BF16), pltpu.VMEM((2, min(FFN_SUB, tm), d), F32)],
        compiler_params=_params("arbitrary", "arbitrary"),
        name="half_ffn",
    )(*args)
    return outs[0] if cast_next is None else tuple(outs)


def _even_inproj_kernel(h_ref, mod_ref, g_ref, win_ref, qg_ref, kg_ref, vg_ref, cos_ref, sin_ref,
                        ones_ref, q_ref, k_ref, v_ref, u_ref, vv_ref, z_ref):
    sub = z_ref.shape[1]
    nsub = h_ref.shape[0] // sub
    ones = ones_ref[...]
    cq, ck, cv, cu, cb = 0, MIX_HALF, MIX_HALF + A_KV, MIX_HALF + 2 * A_KV, 2 * MIX_HALF + 2 * A_KV
    slot = lambda t: lax.rem(pl.program_id(1) + t, 2)

    def norm(t):
        z_ref[slot(t)] = _norm_mod(h_ref[t * sub:(t + 1) * sub, :], g_ref[...], mod_ref[3:4, :],
                                   mod_ref[4:5, :]).astype(BF16)

    norm(0)
    for t in range(nsub):
        rows = slice(t * sub, (t + 1) * sub)
        z = z_ref[slot(t)]
        cos = cos_ref[rows, :]
        sin = sin_ref[rows, :]
        q = _dot(z, win_ref[:, cq:cq + MIX_HALF])
        k = _dot(z, win_ref[:, ck:ck + A_KV])
        if t + 1 < nsub:
            norm(t + 1)
        q = _rope(_head_rms(q, qg_ref[...], ones), cos, sin)
        q_ref[rows, :] = (q * Q_SCALE).astype(BF16)
        bv = jax.nn.gelu(_dot(z, win_ref[:, cb:cb + MIX_HALF]))
        k_ref[rows, :] = _rope(_head_rms(k, kg_ref[...], ones), cos, sin).astype(BF16)
        v_ref[rows, :] = _dot(z, win_ref[:, cv:cv + A_KV]).astype(BF16)
        u = _dot(z, win_ref[:, cu:cu + MIX_HALF])
        vv_ref[rows, :] = _head_rms(bv, vg_ref[...], ones).astype(BF16)
        u_ref[rows, :] = jax.nn.gelu(u)


def _even_inproj(h, mod, mod_row_of_batch, g, win, qg, kg, vg, cos, sin, ones):
    b, s, d = h.shape
    tm = min(IN_TILE, s)
    tok = lambda w: pl.BlockSpec((None, tm, w), lambda bi, i: (bi, i, 0))
    const = lambda shape: _resident(shape, lambda bi, i: (0,) * len(shape))
    return pl.pallas_call(
        _even_inproj_kernel,
        grid=(b, s // tm),
        in_specs=[
            tok(d),
            pl.BlockSpec((None, N_MOD, d), lambda bi, i: (mod_row_of_batch(bi), 0, 0)),
            const((1, d)), const((d, EVEN_IN)), const((1, MIX_HALF)), const((1, A_KV)),
            const((1, MIX_HALF)),
            pl.BlockSpec((tm, LANES), lambda bi, i: (i, 0)),
            pl.BlockSpec((tm, LANES), lambda bi, i: (i, 0)),
            const((MXU_DIM, MXU_DIM)),
        ],
        out_specs=[tok(MIX_HALF), tok(A_KV), tok(A_KV), tok(MIX_HALF), tok(MIX_HALF)],
        out_shape=[
            jax.ShapeDtypeStruct((b, s, MIX_HALF), BF16),
            jax.ShapeDtypeStruct((b, s, A_KV), BF16),
            jax.ShapeDtypeStruct((b, s, A_KV), BF16),
            jax.ShapeDtypeStruct((b, s, MIX_HALF), F32),
            jax.ShapeDtypeStruct((b, s, MIX_HALF), BF16),
        ],
        scratch_shapes=[pltpu.VMEM((2, min(IN_SUB, tm), d), BF16)],
        compiler_params=_params("parallel", "arbitrary"),
        name="even_inproj",
    )(h, mod, g.reshape(1, d), win, qg, kg, vg, cos, sin, ones)


def _even_mixer_kernel(*refs, local, seq):
    if local:
        (q_ref, k_ref, v_ref, kc_ref, vc_ref, sink_ref, u_ref, vv_ref, ws_ref, gb_ref, wout_ref,
         mod_ref, h_ref, o_ref, mix_ref) = refs
    else:
        (q_ref, kc_ref, vc_ref, sink_ref, u_ref, vv_ref, ws_ref, gb_ref, wout_ref,
         mod_ref, h_ref, o_ref, mix_ref) = refs
    tq = q_ref.shape[0]
    nblk = tq // A_BLOCK
    nloc = 3 * A_BLOCK
    lane = lax.broadcasted_iota(jnp.int32, (1, LANES), 1)
    low = lane < HEAD_DIM
    kc = kc_ref[...]
    vc = vc_ref[...]
    groups = N_HEADS // A_KV_HEADS

    def scores(blk):
        r0 = blk * A_BLOCK
        vvb = vv_ref[r0:r0 + B_CHUNK, :]
        for p in range(MIX_HALF // LANES):
            cols = slice(p * LANES, (p + 1) * LANES)
            vsl = vvb[:, cols]
            mixed = jnp.where(low, _dot(ws_ref[2 * p], vsl), _dot(ws_ref[2 * p + 1], vsl)) + gb_ref[:, cols]
            mix_ref[r0:r0 + B_CHUNK, MIX_HALF + p * LANES:MIX_HALF + (p + 1) * LANES] = (
                u_ref[r0:r0 + B_CHUNK, cols] * mixed).astype(BF16)
        qb = q_ref[r0:r0 + A_BLOCK, :]
        q8 = jnp.concatenate(
            [jnp.where(low if j == 0 else jnp.logical_not(low), qb[:, p * LANES:(p + 1) * LANES],
                       jnp.zeros((), BF16))
             for j in range(A_KV_HEADS) for p in range(groups)], axis=0)
        if not local:
            return _dot_nt(q8, kc), vc
        n = pl.program_id(1) * nblk + blk
        start = pl.multiple_of(jnp.clip((n - 1) * A_BLOCK, 0, seq - nloc), A_BLOCK)
        kcat = jnp.concatenate([k_ref[pl.ds(start, nloc), :], kc], axis=0)
        vcat = jnp.concatenate([v_ref[pl.ds(start, nloc), :], vc], axis=0)
        qpos = n * A_BLOCK + lax.broadcasted_iota(jnp.int32, (A_BLOCK, 1), 0)
        kpos = start + lax.broadcasted_iota(jnp.int32, (1, nloc), 1)
        mask = jnp.where(jnp.abs(qpos - kpos) <= WINDOW, 0.0, NEG)
        s = _dot_nt(q8, kcat)
        s_loc = (s[:, :nloc].reshape(N_HEADS, A_BLOCK, nloc) + mask[None]).reshape(N_HEADS * A_BLOCK, nloc)
        return jnp.concatenate([s_loc, s[:, nloc:]], axis=1), vcat

    def values(blk, sv):
        r0 = blk * A_BLOCK
        o = _softmax_pv(sv[0], sv[1], sink_ref[...], sum_on_mxu=not local)
        for p in range(groups):
            lo_rows = o[p * A_BLOCK:(p + 1) * A_BLOCK]
            hi_rows = o[(groups + p) * A_BLOCK:(groups + p + 1) * A_BLOCK]
            mix_ref[r0:r0 + A_BLOCK, p * LANES:(p + 1) * LANES] = jnp.where(low, lo_rows, hi_rows).astype(BF16)

    _pipelined(nblk, QK_AHEAD, scores, values)
    y = _dot(mix_ref[...], wout_ref[...])
    o_ref[...] = h_ref[...] + mod_ref[5:6, :] * y


def _even_mixer(q, k, v, kc, vc, sink_cols, u, vv, ws, gb, wout, mod, mod_row_of_batch, h, local):
    b, s, d = h.shape
    lc = kc.shape[1]
    tq = min(EVEN_TQ, s)
    tok = lambda w: pl.BlockSpec((None, tq, w), lambda bi, i: (bi, i, 0))
    per_batch = lambda rows, w: pl.BlockSpec((None, rows, w), lambda bi, i: (bi, 0, 0))
    const = lambda shape: _resident(shape, lambda bi, i: (0,) * len(shape))
    in_specs = [tok(MIX_HALF)]
    args = [q]
    if local:
        in_specs += [per_batch(s, A_KV), per_batch(s, A_KV)]
        args += [k, v]
    in_specs += [
        per_batch(lc, A_KV), per_batch(lc, A_KV),
        const(sink_cols.shape),
        tok(MIX_HALF), tok(MIX_HALF),
        const(ws.shape), const(gb.shape), const((d, d)),
        pl.BlockSpec((None, N_MOD, d), lambda bi, i: (mod_row_of_batch(bi), 0, 0)),
        tok(d),
    ]
    args += [kc, vc, sink_cols, u, vv, ws, gb, wout, mod, h]
    return pl.pallas_call(
        functools.partial(_even_mixer_kernel, local=local, seq=s),
        grid=(b, s // tq),
        in_specs=in_specs,
        out_specs=tok(d),
        out_shape=jax.ShapeDtypeStruct((b, s, d), F32),
        scratch_shapes=[pltpu.VMEM((tq, d), BF16)],
        compiler_params=_params("parallel", "parallel"),
        name="even_mixer" if local else "even_mixer_ctx",
    )(*args)


def _store_key_windows(dst_ref, dst0, x):
    nmb = len(NA_KSTART)
    for r in range(x.shape[0] // GRID_W):
        jb, r4 = divmod(r, NA_WROWS)
        for m in range(nmb):
            d0 = dst0 + (jb * nmb * NA_WROWS + m * NA_WROWS + r4) * NA_KCOLS
            s0 = r * GRID_W + NA_KSTART[m]
            dst_ref[d0:d0 + NA_KCOLS, :] = x[s0:s0 + NA_KCOLS, :].astype(BF16)


def _odd_inproj_kernel(h_ref, mod_ref, g_ref, win_ref, qg_ref, kg_ref, ones_ref, *refs, kv_only):
    out_refs, z_ref = refs[:-1], refs[-1]
    sub = z_ref.shape[1]
    nsub = h_ref.shape[0] // sub
    ones = ones_ref[...]
    w = MIX_HALF
    slot = lambda t: lax.rem(pl.program_id(1) + t, 2)

    def norm(t):
        z_ref[slot(t)] = _norm_mod(h_ref[t * sub:(t + 1) * sub, :], g_ref[...], mod_ref[3:4, :],
                                   mod_ref[4:5, :]).astype(BF16)

    expand = NA_KCOLS * len(NA_KSTART) // GRID_W
    norm(0)
    for t in range(nsub):
        rows = slice(t * sub, (t + 1) * sub)
        z = z_ref[slot(t)]
        if kv_only:
            k_ref, v_ref = out_refs
            k = _dot(z, win_ref[:, 2 * w:3 * w])
        else:
            xp_ref, q_ref, kw_ref, vw_ref = out_refs
            q = _dot(z, win_ref[:, w:2 * w])
            k = _dot(z, win_ref[:, 2 * w:3 * w])
        if t + 1 < nsub:
            norm(t + 1)
        if not kv_only:
            q_ref[rows, :] = (_head_rms(q, qg_ref[...], ones) * Q_SCALE).astype(BF16)
            xp_ref[rows, :] = _dot(z, win_ref[:, 0:w])
        v = _dot(z, win_ref[:, 3 * w:4 * w])
        k = _head_rms(k, kg_ref[...], ones)
        if kv_only:
            k_ref[rows, :] = k.astype(BF16)
            v_ref[rows, :] = v.astype(BF16)
        else:
            _store_key_windows(kw_ref, t * sub * expand, k)
            _store_key_windows(vw_ref, t * sub * expand, v)


def _odd_inproj(h, mod, mod_row_of_batch, g, win, qg, kg, ones, kv_only):
    b, s, d = h.shape
    tm = min(IN_TILE, s)
    sub = min(IN_SUB, tm)
    expand = NA_KCOLS * len(NA_KSTART) // GRID_W
    tok = lambda w, e=1: pl.BlockSpec((None, e * tm, w), lambda bi, i: (bi, i, 0))
    const = lambda shape: _resident(shape, lambda bi, i: (0,) * len(shape))
    bf = lambda e=1: jax.ShapeDtypeStruct((b, e * s, MIX_HALF), BF16)
    if kv_only:
        out_specs, out_shape = [tok(MIX_HALF)] * 2, [bf(), bf()]
    else:
        assert sub % (NA_WROWS * GRID_W) == 0
        out_specs = [tok(MIX_HALF), tok(MIX_HALF), tok(MIX_HALF, expand), tok(MIX_HALF, expand)]
        out_shape = [jax.ShapeDtypeStruct((b, s, MIX_HALF), F32), bf(), bf(expand), bf(expand)]
    return pl.pallas_call(
        functools.partial(_odd_inproj_kernel, kv_only=kv_only),
        grid=(b, s // tm),
        in_specs=[
            tok(d),
            pl.BlockSpec((None, N_MOD, d), lambda bi, i: (mod_row_of_batch(bi), 0, 0)),
            const((1, d)), const((d, ODD_IN)), const((1, MIX_HALF)), const((1, MIX_HALF)),
            const((MXU_DIM, MXU_DIM)),
        ],
        out_specs=out_specs,
        out_shape=out_shape,
        scratch_shapes=[pltpu.VMEM((2, sub, d), BF16)],
        compiler_params=_params("parallel", "arbitrary"),
        name="odd_inproj_kv" if kv_only else "odd_inproj",
    )(h, mod, g.reshape(1, d), win, qg, kg, ones)


def _na_tables(rows):
    nrb = rows // NA_QROWS
    nc = 2 * NA_COLS - 1
    ridx = np.zeros((3, NA_QROWS, NA_KROWS), np.int32)
    for rv, a in enumerate((0, nrb // 2, nrb - 1)):
        ks = int(np.clip(NA_QROWS * a - NA_ROWS // 2, 0, rows - NA_KROWS))
        r = NA_QROWS * a + np.arange(NA_QROWS)[:, None]
        kr = ks + np.arange(NA_KROWS)[None, :]
        r0 = np.clip(r - NA_ROWS // 2, 0, rows - NA_ROWS)
        ridx[rv] = np.where((kr >= r0) & (kr < r0 + NA_ROWS), kr - r + NA_ROWS - 1, -1)
    csel = np.zeros((3, NA_COLS, NA_KCOLS, nc), np.float32)
    cvalid = np.zeros((3, NA_COLS, NA_KCOLS), bool)
    for cv, m in enumerate((0, 1, 3)):
        qc = NA_COLS * m + np.arange(NA_COLS)[:, None]
        kcol = NA_KSTART[m] + np.arange(NA_KCOLS)[None, :]
        qstart = np.clip(qc - NA_COLS // 2, 0, GRID_W - NA_COLS)
        cvalid[cv] = (kcol >= qstart) & (kcol < qstart + NA_COLS)
        ci = np.clip(kcol - qc, 1 - NA_COLS, NA_COLS - 1) + NA_COLS - 1
        csel[cv] = np.arange(nc) == ci[..., None]
    return ridx, csel, cvalid


def _na_bias_table(rpb, rows):
    ridx, csel, cvalid = _na_tables(rows)
    nr = 2 * NA_ROWS - 1
    nk = NA_KROWS * NA_KCOLS
    cb = jnp.einsum("hab,cqjb->chqaj", rpb, csel, precision=lax.Precision.HIGHEST)
    cb = jnp.where(cvalid[:, None, :, None, :], cb * LOG2E, NEG)
    flat = cb.reshape(cb.shape[:3] + (nr * NA_KCOLS,))
    run = NA_ROWS * NA_KCOLS
    outside = jnp.full(cb.shape[:3] + (nk - run,), NEG, F32)
    variants = []
    for rv in range(ridx.shape[0]):
        strips = []
        for rl in range(NA_QROWS):
            kl0 = int(np.argmax(ridx[rv, rl] >= 0))
            a0 = int(ridx[rv, rl, kl0])
            assert (ridx[rv, rl, kl0:kl0 + NA_ROWS] == a0 + np.arange(NA_ROWS)).all()
            assert (np.delete(ridx[rv, rl], np.s_[kl0:kl0 + NA_ROWS]) < 0).all()
            strips.append(jnp.concatenate([outside[..., :kl0 * NA_KCOLS],
                                           flat[..., a0 * NA_KCOLS:a0 * NA_KCOLS + run],
                                           outside[..., kl0 * NA_KCOLS:]], axis=-1))
        variants.append(jnp.stack(strips, axis=2))
    tbl = jnp.stack(variants, axis=0)
    return tbl.reshape(3, 3, rpb.shape[0], NA_QROWS * NA_COLS, nk)


def _odd_mixer_kernel(xp_ref, xprev_ref, xnext_ref, q_ref, *refs, seq):
    nwb = NA_KROWS // NA_WROWS
    kw_refs, vw_refs = refs[:nwb], refs[nwb:2 * nwb]
    (kc_ref, vc_ref, bias_ref, wpool_ref, cs_ref, wout_ref, mod_ref, h_ref, o_ref, mix_ref,
     xw_ref, ta_ref, tb_ref) = refs[2 * nwb:]
    a = pl.program_id(1)
    last = pl.num_programs(1) - 1
    tq = q_ref.shape[0]
    halo = xprev_ref.shape[0]
    lane = lax.broadcasted_iota(jnp.int32, (1, LANES), 1)
    low = lane < HEAD_DIM

    xw_ref[0:halo, :] = jnp.where(a == 0, 0.0, xprev_ref[...])
    xw_ref[halo:halo + tq, :] = xp_ref[...]
    xw_ref[halo + tq:halo + tq + halo, :] = jnp.where(a == last, 0.0, xnext_ref[...])
    n = tq + 2 * halo
    xw_ref[n:n + halo, :] = jnp.zeros((halo, MIX_HALF), F32)
    ta_ref[n:n + halo, :] = jnp.zeros((halo, C_GROUP_W), F32)
    tb_ref[n:n + halo, :] = jnp.zeros((halo, C_GROUP_W), F32)
    t = a * tq + lax.broadcasted_iota(jnp.int32, (tq, 1), 0)

    def window_sum(cols, w):
        lo = halo - w // 2
        if w < 8:
            acc = None
            for dlt in range(w):
                term = xw_ref[lo + dlt:lo + dlt + tq, cols]
                acc = term if acc is None else acc + term
            return acc
        assert w in (8, 16) and halo == 8
        ta_ref[0:n, :] = xw_ref[0:n, cols] + xw_ref[1:n + 1, cols]
        tb_ref[0:n, :] = ta_ref[0:n, :] + ta_ref[2:n + 2, :]
        ta_ref[0:n, :] = tb_ref[0:n, :] + tb_ref[4:n + 4, :]
        if w == 8:
            return ta_ref[lo:lo + tq, :]
        return ta_ref[0:tq, :] + ta_ref[8:8 + tq, :]

    def pool(gi):
        w = POOL_WINDOWS[gi]
        cols = slice(gi * C_GROUP_W, (gi + 1) * C_GROUP_W)
        acc = window_sum(cols, w)
        cnt =(jnp.minimum(t + (w - w // 2), seq) - jnp.maximum(t - w // 2, 0)).astype(F32)
        pooled = acc / cnt - xp_ref[:, cols]
        y = _dot(pooled.astype(BF16), wpool_ref[gi]) * cs_ref[:, cols]
        mix_ref[:, cols] = y.astype(BF16)

    nmb = GRID_W // NA_COLS
    wrows = NA_WROWS * NA_KCOLS
    nloc = NA_KROWS * NA_KCOLS

    def scores(i):
        p, m = divmod(i, nmb)
        cols = slice(p * LANES, (p + 1) * LANES)
        win = slice(m * wrows, (m + 1) * wrows)
        kcat = jnp.concatenate([r[win, cols] for r in kw_refs] + [kc_ref[:, cols]], axis=0)
        vcat = jnp.concatenate([r[win, cols] for r in vw_refs] + [vc_ref[:, cols]], axis=0)
        qm = jnp.concatenate(
            [q_ref[r * GRID_W + m * NA_COLS:r * GRID_W + (m + 1) * NA_COLS, cols]
             for r in range(NA_QROWS)], axis=0)
        zero = jnp.zeros((), BF16)
        q2 = jnp.concatenate([jnp.where(low, qm, zero), jnp.where(low, zero, qm)], axis=0)
        s = _dot_nt(q2, kcat)
        bias = jnp.concatenate([bias_ref[NA_COLVAR[m], 2 * p], bias_ref[NA_COLVAR[m], 2 * p + 1]], axis=0)
        return jnp.concatenate([s[:, :nloc] + bias, s[:, nloc:]], axis=1), vcat

    def values(i, sv):
        p, m = divmod(i, nmb)
        nq = NA_QROWS * NA_COLS
        o = _softmax_pv(sv[0], sv[1])
        o = jnp.where(low, o[:nq], o[nq:]).astype(BF16)
        for r in range(NA_QROWS):
            mix_ref[r * GRID_W + m * NA_COLS:r * GRID_W + (m + 1) * NA_COLS,
                    MIX_HALF + p * LANES:MIX_HALF + (p + 1) * LANES] = o[r * NA_COLS:(r + 1) * NA_COLS]

    for gi in range(len(POOL_WINDOWS)):
        pool(gi)
    _pipelined((MIX_HALF // LANES) * nmb, QK_AHEAD, scores, values)
    y = _dot(mix_ref[...], wout_ref[...])
    o_ref[...] = h_ref[...] + mod_ref[5:6, :] * y


def _odd_mixer(xp, q, kw, vw, kc, vc, bias_tbl, wpool, cscale, wout, mod, h):
    b, s, d = h.shape
    lc = kc.shape[1]
    tq = NA_QROWS * GRID_W
    nrb = s // tq
    halo = 8
    hb = tq // halo
    nwb = NA_KROWS // NA_WROWS
    wblock = NA_WROWS * NA_KCOLS * len(NA_KSTART)
    tok = lambda w: pl.BlockSpec((None, tq, w), lambda bi, i: (bi, i, 0))
    per_batch = lambda rows, w: _resident((None, rows, w), lambda bi, i: (bi, 0, 0))
    const = lambda shape: _resident(shape, lambda bi, i: (0,) * len(shape))

    def row_variant(i):
        return jnp.where(i == 0, 0, jnp.where(i == nrb - 1, 2, 1))

    def window(j):
        first = lambda i: jnp.clip((NA_QROWS * i - NA_ROWS // 2) // NA_WROWS, 0,
                                   (s // GRID_W - NA_KROWS) // NA_WROWS)
        return pl.BlockSpec((None, wblock, MIX_HALF), lambda bi, i: (bi, first(i) + j, 0))

    return pl.pallas_call(
        functools.partial(_odd_mixer_kernel, seq=s),
        grid=(b, nrb),
        in_specs=[
            tok(MIX_HALF),
            pl.BlockSpec((None, halo, MIX_HALF), lambda bi, i: (bi, jnp.maximum(i * hb - 1, 0), 0)),
            pl.BlockSpec((None, halo, MIX_HALF), lambda bi, i: (bi, jnp.minimum((i + 1) * hb, s // halo - 1), 0)),
            tok(MIX_HALF),
            *[window(j) for j in range(nwb)], *[window(j) for j in range(nwb)],
            per_batch(lc, MIX_HALF), per_batch(lc, MIX_HALF),
            pl.BlockSpec((None,) + bias_tbl.shape[1:], lambda bi, i: (row_variant(i), 0, 0, 0, 0)),
            const(wpool.shape), const(cscale.shape), const((d, d)),
            pl.BlockSpec((None, N_MOD, d), lambda bi, i: (bi, 0, 0)),
            tok(d),
        ],
        out_specs=tok(d),
        out_shape=jax.ShapeDtypeStruct((b, s, d), F32),
        scratch_shapes=[pltpu.VMEM((tq, d), BF16), pltpu.VMEM((tq + 3 * halo, MIX_HALF), F32),
                        pltpu.VMEM((tq + 3 * halo, C_GROUP_W), F32), pltpu.VMEM((tq + 3 * halo, C_GROUP_W), F32)],
        compiler_params=_params("parallel", "arbitrary"),
        name="odd_mixer",
    )(xp, xp, xp, q, *([kw] * nwb), *([vw] * nwb), kc, vc, bias_tbl, wpool, cscale, wout, mod, h)


def _rope_tables(s):
    t = jnp.arange(s)
    row = (t // GRID_W).astype(F32)
    col = (t % GRID_W).astype(F32)
    m = HEAD_DIM // 4
    inv = 1.0 / (ROPE_THETA ** (jnp.arange(m, dtype=F32) / m))
    ang_r = row[:, None] * inv[None, :]
    ang_c = col[:, None] * inv[None, :]
    cos_h = jnp.concatenate([jnp.cos(ang_r)] * 2 + [jnp.cos(ang_c)] * 2, axis=-1)
    sin_h = jnp.concatenate([-jnp.sin(ang_r), jnp.sin(ang_r), -jnp.sin(ang_c), jnp.sin(ang_c)], axis=-1)
    reps = LANES // HEAD_DIM
    return jnp.tile(cos_h, (1, reps)), jnp.tile(sin_h, (1, reps))


def _block_diag_ones():
    idx = np.arange(MXU_DIM) // HEAD_DIM
    return jnp.asarray(idx[:, None] == idx[None, :], dtype=BF16)


def kernel(x, c, ctx, c_ctx, ada_w, ada_b, norm_g, ffn_w_gu, ffn_w_down, ev_w_in, ev_w_out,
           a_q_gain, a_k_gain, a_sink, b_v_gain, b_ws, b_bias, od_w_in, od_w_out, c_w_pool,
           c_scale, d_q_gain, d_k_gain, d_rpb):
    b, s, d = x.shape
    lc = ctx.shape[1]
    depth = ada_w.shape[0]
    assert d == D_MODEL and depth == 2 and s % TOK_TILE == 0 and s // GRID_W >= NA_KROWS
    assert lc % B_CHUNK == 0 and b <= 8

    mod_rows = 16
    cvec = jnp.zeros((mod_rows, d), F32).at[:b].set(c).at[b].set(c_ctx)
    mod = _modulation(cvec, ada_w, ada_b)
    lat_row = lambda bi: bi
    ctx_row = lambda bi: b

    ones = _block_diag_ones()
    cos, sin = _rope_tables(s)
    flat = lambda t: t.reshape(1, b * lc, t.shape[-1])
    unflat = lambda t: t.reshape(b, lc, t.shape[-1])
    cos_id = jnp.ones((b * lc, LANES), F32)
    sin_id = jnp.zeros((b * lc, LANES), F32)
    tile_heads = lambda gain, n: jnp.tile(gain, n).reshape(1, n * HEAD_DIM)

    wgu = ffn_w_gu[0, 0].astype(BF16)
    wd = ffn_w_down[0, 0].astype(BF16)
    cast = lambda nl, nh: (ffn_w_gu, ffn_w_down, nl, nh)
    h, hc = x, flat(ctx)
    li, e = 0, 0
    h, wgu_next, wd_next = _half_ffn(h, mod[li], lat_row, norm_g[li, 0], wgu, wd, 0, cast(li, 1))
    hc = _half_ffn(hc, mod[li], ctx_row, norm_g[li, 0], wgu, wd, 0)
    hsl = lambda hd: slice(hd * HEAD_DIM, (hd + 1) * HEAD_DIM)
    win = jnp.concatenate([ev_w_in[e][:, hsl(hd)] for hd in A_HEAD_PERM] + [ev_w_in[e][:, MIX_HALF:]],
                          axis=1).astype(BF16)
    wout = jnp.concatenate([ev_w_out[e][hsl(hd)] for hd in A_HEAD_PERM] + [ev_w_out[e][MIX_HALF:]],
                           axis=0).astype(BF16)
    qg = tile_heads(a_q_gain[e], N_HEADS)
    kg = tile_heads(a_k_gain[e], A_KV_HEADS)
    vg = b_v_gain[e].reshape(1, MIX_HALF)
    q, k, v, u, vv = _even_inproj(h, mod[li], lat_row, norm_g[li, 1], win, qg, kg, vg, cos, sin, ones)
    qc, kc, vc, uc, vvc = map(unflat, _even_inproj(hc, mod[li], ctx_row, norm_g[li, 1], win, qg, kg, vg,
                                                   cos_id, sin_id, ones))
    sink_cols = jnp.repeat(a_sink[e] * LOG2E, A_BLOCK).reshape(N_HEADS * A_BLOCK, 1)
    ws = b_ws[e].astype(BF16)
    gb = jnp.repeat(b_bias[e].T, HEAD_DIM, axis=1)
    h = _even_mixer(q, k, v, kc, vc, sink_cols, u, vv, ws, gb, wout, mod[li], lat_row, h, True)
    hc = flat(_even_mixer(qc, None, None, kc, vc, sink_cols, uc, vvc, ws, gb, wout, mod[li], ctx_row,
                          unflat(hc), False))
    wgu, wd = wgu_next, wd_next
    h, wgu_next, wd_next = _half_ffn(h, mod[li], lat_row, norm_g[li, 2], wgu, wd, 1, cast(li + 1, 0))
    hc = _half_ffn(hc, mod[li], ctx_row, norm_g[li, 2], wgu, wd, 1)

    li, o = 1, 0
    wgu, wd = wgu_next, wd_next
    h, wgu_next, wd_next = _half_ffn(h, mod[li], lat_row, norm_g[li, 0], wgu, wd, 0, cast(li, 1))
    hc = _half_ffn(hc, mod[li], ctx_row, norm_g[li, 0], wgu, wd, 0)
    win = od_w_in[o].astype(BF16)
    wout = od_w_out[o].astype(BF16)
    qg = tile_heads(d_q_gain[o], N_HEADS)
    kg = tile_heads(d_k_gain[o], N_HEADS)
    xp, q, kw, vw = _odd_inproj(h, mod[li], lat_row, norm_g[li, 1], win, qg, kg, ones, False)
    kc, vc = map(unflat, _odd_inproj(hc, mod[li], ctx_row, norm_g[li, 1], win, qg, kg, ones, True))
    bias_tbl = _na_bias_table(d_rpb[o], s // GRID_W)
    h = _odd_mixer(xp, q, kw, vw, kc, vc, bias_tbl, c_w_pool[o].astype(BF16), c_scale[o].reshape(1, MIX_HALF),
                   wout, mod[li], h)
    h = _half_ffn(h, mod[li], lat_row, norm_g[li, 2], wgu_next, wd_next, 1)
    return h
```

```python
import functools

import numpy as np
import jax
import jax.numpy as jnp
from jax import lax
from jax.experimental import pallas as pl
from jax.experimental.pallas import tpu as pltpu

F32 = jnp.float32
BF16 = jnp.bfloat16

D_MODEL = 1024
GRID_W = 64
HEAD_DIM = 64
MIX_HALF = D_MODEL // 2
N_MOD = 9
D_FF = ((8 * D_MODEL // 3 + 127) // 128) * 128
EPS = 1e-6
ROPE_THETA = 10000.0
NEG = -1e30
N_HEADS = MIX_HALF // HEAD_DIM
A_KV_HEADS = N_HEADS // 4
A_KV = A_KV_HEADS * HEAD_DIM
WINDOW = 128
A_BLOCK = 128
B_CHUNK = 128
EVEN_IN = MIX_HALF + 2 * A_KV + 2 * MIX_HALF
POOL_WINDOWS = (2, 4, 8, 16)
C_GROUP_W = MIX_HALF // len(POOL_WINDOWS)
NA_ROWS = 8
NA_COLS = 16
ODD_IN = 4 * MIX_HALF

LANES = 128
MXU_DIM = 256
VMEM_LIMIT = 56 * 1024 * 1024

LOG2E = 1.4426950408889634
Q_SCALE = HEAD_DIM ** -0.5 * LOG2E
QK_AHEAD = 2
TOK_TILE = 512
FFN_TILE = 1024
FFN_SUB = 512
EVEN_TQ = 512
NA_QROWS = 8
NA_KROWS = 16
NA_KCOLS = 2 * NA_COLS
NA_WROWS = 4
IN_TILE = 1024
IN_SUB = 512
NA_KSTART = tuple(int(v) for v in np.clip(np.arange(GRID_W // NA_COLS) * NA_COLS - NA_COLS // 2,
                                         0, GRID_W - NA_KCOLS))
NA_COLVAR = (0, 1, 1, 2)
FFN_CHUNKS = (512, 512, 512, 512, 512, 256)
assert sum(FFN_CHUNKS) == D_FF

A_HEAD_PERM = (0, 4, 1, 5, 2, 6, 3, 7)


def _params(*sem):
    return pltpu.CompilerParams(dimension_semantics=sem, vmem_limit_bytes=VMEM_LIMIT)


def _resident(shape, index_map):
    return pl.BlockSpec(shape, index_map, pipeline_mode=pl.Buffered(1))


def _norm_mod(h, g, shift, scale):
    y = h * lax.rsqrt(jnp.mean(h * h, axis=-1, keepdims=True) + EPS)
    return (y * g) * (1.0 + scale) + shift


def _dot(a, b):
    return jnp.dot(a, b, preferred_element_type=F32)


def _dot_nt(a, b):
    return lax.dot_general(a, b, (((1,), (1,)), ((), ())), preferred_element_type=F32)


def _head_rms(x, gain, ones_bd):
    w = x.shape[1]
    step = min(w, MXU_DIM)
    ones = ones_bd[:step, :step]
    parts = []
    for c in range(0, w, step):
        xs = x[:, c:c + step]
        ss = _dot((xs * xs).astype(BF16), ones)
        parts.append(xs * lax.rsqrt(ss / HEAD_DIM + EPS))
    y = parts[0] if len(parts) == 1 else jnp.concatenate(parts, axis=1)
    return y * gain


def _rope(x, cos, sin_signed):
    lane = lax.broadcasted_iota(jnp.int32, (1, LANES), 1)
    first = (lane & 16) == 0
    parts = []
    for c in range(0, x.shape[1], LANES):
        xb = x[:, c:c + LANES]
        partner = jnp.where(first, pltpu.roll(xb, LANES - 16, 1), pltpu.roll(xb, 16, 1))
        parts.append(xb * cos + partner * sin_signed)
    return parts[0] if len(parts) == 1 else jnp.concatenate(parts, axis=1)


def _softmax_pv(s, vcat, sink=None, sum_on_mxu=False):
    m = jnp.max(s, axis=-1, keepdims=True)
    if sink is not None:
        m = jnp.maximum(m, sink)
    p = jnp.exp2(s - m)
    if sum_on_mxu:
        o = _dot(p.astype(BF16), jnp.concatenate([vcat, jnp.ones(vcat.shape, BF16)], axis=1))
        num, den = o[:, :LANES], o[:, LANES:]
    else:
        den = jnp.sum(p, axis=-1, keepdims=True)
        num = _dot(p.astype(BF16), vcat)
    if sink is not None:
        den = den + jnp.exp2(sink - m)
    return num / den


def _pipelined(n, ahead, stage_a, stage_b):
    pending = [stage_a(i) for i in range(min(ahead, n))]
    for i in range(n):
        if i + ahead < n:
            pending.append(stage_a(i + ahead))
        stage_b(i, pending.pop(0))


def _mod_kernel(c_ref, w_ref, b_ref, o_ref):
    sc = jax.nn.silu(c_ref[...])
    o_ref[...] = _dot(sc, w_ref[...]) + b_ref[...]


def _modulation(cvec, ada_w, ada_b):
    depth, d, n = ada_w.shape
    r = cvec.shape[0]
    tn = 3072
    out = pl.pallas_call(
        _mod_kernel,
        grid=(depth, n // tn),
        in_specs=[
            pl.BlockSpec((r, d), lambda l, j: (0, 0)),
            pl.BlockSpec((None, d, tn), lambda l, j: (l, 0, j)),
            pl.BlockSpec((None, 1, tn), lambda l, j: (l, 0, j)),
        ],
        out_specs=pl.BlockSpec((None, r, tn), lambda l, j: (l, 0, j)),
        out_shape=jax.ShapeDtypeStruct((depth, r, n), F32),
        compiler_params=_params("parallel", "parallel"),
        name="adaln_mod",
    )(cvec, ada_w, ada_b.reshape(depth, 1, n))
    return out.reshape(depth, r, N_MOD, d)


def _ffn_kernel(h_ref, mod_ref, g_ref, wgu_ref, wd_ref, *refs, mrow, cast_next):
    if cast_next:
        src_gu_ref, src_d_ref, o_ref, dst_gu_ref, dst_d_ref, z_ref, acc_ref = refs
        dst_gu_ref[...] = src_gu_ref[...].astype(BF16)
        dst_d_ref[...] = src_d_ref[...].astype(BF16)
    else:
        o_ref, z_ref, acc_ref = refs
    sub = z_ref.shape[1]
    nsub = h_ref.shape[0] // sub
    slot = lambda t: lax.rem(pl.program_id(1) + t, 2)

    def norm(t):
        h = h_ref[t * sub:(t + 1) * sub, :]
        z_ref[slot(t)] = _norm_mod(h, g_ref[...], mod_ref[mrow:mrow + 1, :],
                                   mod_ref[mrow + 1:mrow + 2, :]).astype(BF16)

    def residual(t):
        rows = slice(t * sub, (t + 1) * sub)
        o_ref[rows, :] = h_ref[rows, :] + (0.5 * mod_ref[mrow + 2:mrow + 3, :]) * acc_ref[slot(t)]

    norm(0)
    for t in range(nsub):
        acc = None
        off = 0
        for ci, ch in enumerate(FFN_CHUNKS):
            z = z_ref[slot(t)]
            gt = _dot(z, wgu_ref[:, off:off + ch])
            up = _dot(z, wgu_ref[:, D_FF + off:D_FF + off + ch])
            a = (jax.nn.silu(gt) * up).astype(BF16)
            part = _dot(a, wd_ref[off:off + ch, :])
            acc = part if acc is None else acc + part
            off += ch
            if ci == 0 and t >= 1:
                residual(t - 1)
            if ci == 1 and t + 1 < nsub:
                norm(t + 1)
        acc_ref[slot(t)] = acc
    residual(nsub - 1)


def _half_ffn(h, mod, mod_row_of_batch, g, wgu, wd, half, cast_next=None):
    b, s, d = h.shape
    tm = min(FFN_TILE, s)
    nt = s // tm
    mrow = 6 * half
    in_specs = [
        pl.BlockSpec((None, tm, d), lambda bi, i: (bi, i, 0)),
        pl.BlockSpec((None, N_MOD, d), lambda bi, i: (mod_row_of_batch(bi), 0, 0)),
        _resident((1, d), lambda bi, i: (0, 0)),
        _resident((d, 2 * D_FF), lambda bi, i: (0, 0)),
        _resident((D_FF, d), lambda bi, i: (0, 0)),
    ]
    args = [h, mod, g.reshape(1, d), wgu, wd]
    out_specs = [pl.BlockSpec((None, tm, d), lambda bi, i: (bi, i, 0))]
    out_shape = [jax.ShapeDtypeStruct((b, s, d), F32)]
    if cast_next is not None:
        src_gu, src_d, nl, nh = cast_next
        steps = b * nt
        rows_gu = d // steps
        rep = 1
        while (D_FF * rep // steps) % 16:
            rep *= 2
        rows_d = D_FF * rep // steps
        assert rows_gu * steps == d and rows_gu % 16 == 0 and rows_d * steps == D_FF * rep and rep <= steps
        in_specs += [
            pl.BlockSpec((None, None, rows_gu, 2 * D_FF), lambda bi, i: (nl, nh, bi * nt + i, 0)),
            pl.BlockSpec((None, None, rows_d, d), lambda bi, i: (nl, nh, (bi * nt + i) // rep, 0)),
        ]
        args += [src_gu, src_d]
        out_specs += [pl.BlockSpec((rows_gu, 2 * D_FF), lambda bi, i: (bi * nt + i, 0)),
                      pl.BlockSpec((rows_d, d), lambda bi, i: ((bi * nt + i) // rep, 0))]
        out_shape += [jax.ShapeDtypeStruct((d, 2 * D_FF), BF16), jax.ShapeDtypeStruct((D_FF, d), BF16)]
    outs = pl.pallas_call(
        functools.partial(_ffn_kernel, mrow=mrow, cast_next=cast_next is not None),
        grid=(b, nt),
        in_specs=in_specs,
        out_specs=out_specs,
        out_shape=out_shape,
        scratch_shapes=[pltpu.VMEM((2, min(FFN_SUB, tm), d), BF16), pltpu.VMEM((2, min(FFN_SUB, tm), d), F32)],
        compiler_params=_params("arbitrary", "arbitrary"),
        name="half_ffn",
    )(*args)
    return outs[0] if cast_next is None else tuple(outs)


def _even_inproj_kernel(h_ref, mod_ref, g_ref, win_ref, qg_ref, kg_ref, vg_ref, cos_ref, sin_ref,
                        ones_ref, q_ref, k_ref, v_ref, u_ref, vv_ref, z_ref):
    sub = z_ref.shape[1]
    nsub = h_ref.shape[0] // sub
    ones = ones_ref[...]
    cq, ck, cv, cu, cb = 0, MIX_HALF, MIX_HALF + A_KV, MIX_HALF + 2 * A_KV, 2 * MIX_HALF + 2 * A_KV
    slot = lambda t: lax.rem(pl.program_id(1) + t, 2)

    def norm(t):
        z_ref[slot(t)] = _norm_mod(h_ref[t * sub:(t + 1) * sub, :], g_ref[...], mod_ref[3:4, :],
                                   mod_ref[4:5, :]).astype(BF16)

    norm(0)
    for t in range(nsub):
        rows = slice(t * sub, (t + 1) * sub)
        z = z_ref[slot(t)]
        cos = cos_ref[rows, :]
        sin = sin_ref[rows, :]
        q = _dot(z, win_ref[:, cq:cq + MIX_HALF])
        k = _dot(z, win_ref[:, ck:ck + A_KV])
        if t + 1 < nsub:
            norm(t + 1)
        q = _rope(_head_rms(q, qg_ref[...], ones), cos, sin)
        q_ref[rows, :] = (q * Q_SCALE).astype(BF16)
        bv = jax.nn.gelu(_dot(z, win_ref[:, cb:cb + MIX_HALF]))
        k_ref[rows, :] = _rope(_head_rms(k, kg_ref[...], ones), cos, sin).astype(BF16)
        v_ref[rows, :] = _dot(z, win_ref[:, cv:cv + A_KV]).astype(BF16)
        u = _dot(z, win_ref[:, cu:cu + MIX_HALF])
        vv_ref[rows, :] = _head_rms(bv, vg_ref[...], ones).astype(BF16)
        u_ref[rows, :] = jax.nn.gelu(u)


def _even_inproj(h, mod, mod_row_of_batch, g, win, qg, kg, vg, cos, sin, ones):
    b, s, d = h.shape
    tm = min(IN_TILE, s)
    tok = lambda w: pl.BlockSpec((None, tm, w), lambda bi, i: (bi, i, 0))
    const = lambda shape: _resident(shape, lambda bi, i: (0,) * len(shape))
    return pl.pallas_call(
        _even_inproj_kernel,
        grid=(b, s // tm),
        in_specs=[
            tok(d),
            pl.BlockSpec((None, N_MOD, d), lambda bi, i: (mod_row_of_batch(bi), 0, 0)),
            const((1, d)), const((d, EVEN_IN)), const((1, MIX_HALF)), const((1, A_KV)),
            const((1, MIX_HALF)),
            pl.BlockSpec((tm, LANES), lambda bi, i: (i, 0)),
            pl.BlockSpec((tm, LANES), lambda bi, i: (i, 0)),
            const((MXU_DIM, MXU_DIM)),
        ],
        out_specs=[tok(MIX_HALF), tok(A_KV), tok(A_KV), tok(MIX_HALF), tok(MIX_HALF)],
        out_shape=[
            jax.ShapeDtypeStruct((b, s, MIX_HALF), BF16),
            jax.ShapeDtypeStruct((b, s, A_KV), BF16),
            jax.ShapeDtypeStruct((b, s, A_KV), BF16),
            jax.ShapeDtypeStruct((b, s, MIX_HALF), F32),
            jax.ShapeDtypeStruct((b, s, MIX_HALF), BF16),
        ],
        scratch_shapes=[pltpu.VMEM((2, min(IN_SUB, tm), d), BF16)],
        compiler_params=_params("parallel", "arbitrary"),
        name="even_inproj",
    )(h, mod, g.reshape(1, d), win, qg, kg, vg, cos, sin, ones)


def _even_mixer_kernel(*refs, local, seq):
    if local:
        (q_ref, k_ref, v_ref, kc_ref, vc_ref, sink_ref, u_ref, vv_ref, ws_ref, gb_ref, wout_ref,
         mod_ref, h_ref, o_ref, mix_ref) = refs
    else:
        (q_ref, kc_ref, vc_ref, sink_ref, u_ref, vv_ref, ws_ref, gb_ref, wout_ref,
         mod_ref, h_ref, o_ref, mix_ref) = refs
    tq = q_ref.shape[0]
    nblk = tq // A_BLOCK
    nloc = 3 * A_BLOCK
    lane = lax.broadcasted_iota(jnp.int32, (1, LANES), 1)
    low = lane < HEAD_DIM
    kc = kc_ref[...]
    vc = vc_ref[...]
    groups = N_HEADS // A_KV_HEADS

    def gmlp(blk):
        r0 = blk * B_CHUNK
        vvb = vv_ref[r0:r0 + B_CHUNK, :]
        for p in range(MIX_HALF // LANES):
            cols = slice(p * LANES, (p + 1) * LANES)
            vsl = vvb[:, cols]
            mixed = jnp.where(low, _dot(ws_ref[2 * p], vsl), _dot(ws_ref[2 * p + 1], vsl)) + gb_ref[:, cols]
            mix_ref[r0:r0 + B_CHUNK, MIX_HALF + p * LANES:MIX_HALF + (p + 1) * LANES] = (
                u_ref[r0:r0 + B_CHUNK, cols] * mixed).astype(BF16)

    def scores(blk):
        r0 = blk * A_BLOCK
        qb = q_ref[r0:r0 + A_BLOCK, :]
        q8 = jnp.concatenate(
            [jnp.where(low if j == 0 else jnp.logical_not(low), qb[:, p * LANES:(p + 1) * LANES],
                       jnp.zeros((), BF16))
             for j in range(A_KV_HEADS) for p in range(groups)], axis=0)
        if not local:
            return _dot_nt(q8, kc), vc
        n = pl.program_id(1) * nblk + blk
        start = pl.multiple_of(jnp.clip((n - 1) * A_BLOCK, 0, seq - nloc), A_BLOCK)
        kcat = jnp.concatenate([k_ref[pl.ds(start, nloc), :], kc], axis=0)
        vcat = jnp.concatenate([v_ref[pl.ds(start, nloc), :], vc], axis=0)
        qpos = n * A_BLOCK + lax.broadcasted_iota(jnp.int32, (A_BLOCK, 1), 0)
        kpos = start + lax.broadcasted_iota(jnp.int32, (1, nloc), 1)
        mask = jnp.where(jnp.abs(qpos - kpos) <= WINDOW, 0.0, NEG)
        s = _dot_nt(q8, kcat)
        s_loc = (s[:, :nloc].reshape(N_HEADS, A_BLOCK, nloc) + mask[None]).reshape(N_HEADS * A_BLOCK, nloc)
        return jnp.concatenate([s_loc, s[:, nloc:]], axis=1), vcat

    def values(blk, sv):
        r0 = blk * A_BLOCK
        o = _softmax_pv(sv[0], sv[1], sink_ref[...], sum_on_mxu=not local)
        for p in range(groups):
            lo_rows = o[p * A_BLOCK:(p + 1) * A_BLOCK]
            hi_rows = o[(groups + p) * A_BLOCK:(groups + p + 1) * A_BLOCK]
            mix_ref[r0:r0 + A_BLOCK, p * LANES:(p + 1) * LANES] = jnp.where(low, lo_rows, hi_rows).astype(BF16)
        tail = min(QK_AHEAD, nblk)
        assert nblk % tail == 0
        if blk >= nblk - tail:
            per = nblk // tail
            for c in range((blk - (nblk - tail)) * per, (blk - (nblk - tail) + 1) * per):
                gmlp(c)

    _pipelined(nblk, QK_AHEAD, scores, values)
    y = _dot(mix_ref[...], wout_ref[...])
    o_ref[...] = h_ref[...] + mod_ref[5:6, :] * y


def _even_mixer(q, k, v, kc, vc, sink_cols, u, vv, ws, gb, wout, mod, mod_row_of_batch, h, local):
    b, s, d = h.shape
    lc = kc.shape[1]
    tq = min(EVEN_TQ, s)
    tok = lambda w: pl.BlockSpec((None, tq, w), lambda bi, i: (bi, i, 0))
    per_batch = lambda rows, w: pl.BlockSpec((None, rows, w), lambda bi, i: (bi, 0, 0))
    const = lambda shape: _resident(shape, lambda bi, i: (0,) * len(shape))
    in_specs = [tok(MIX_HALF)]
    args = [q]
    if local:
        in_specs += [per_batch(s, A_KV), per_batch(s, A_KV)]
        args += [k, v]
    in_specs += [
        per_batch(lc, A_KV), per_batch(lc, A_KV),
        const(sink_cols.shape),
        tok(MIX_HALF), tok(MIX_HALF),
        const(ws.shape), const(gb.shape), const((d, d)),
        pl.BlockSpec((None, N_MOD, d), lambda bi, i: (mod_row_of_batch(bi), 0, 0)),
        tok(d),
    ]
    args += [kc, vc, sink_cols, u, vv, ws, gb, wout, mod, h]
    return pl.pallas_call(
        functools.partial(_even_mixer_kernel, local=local, seq=s),
        grid=(b, s // tq),
        in_specs=in_specs,
        out_specs=tok(d),
        out_shape=jax.ShapeDtypeStruct((b, s, d), F32),
        scratch_shapes=[pltpu.VMEM((tq, d), BF16)],
        compiler_params=_params("parallel", "parallel"),
        name="even_mixer" if local else "even_mixer_ctx",
    )(*args)


def _store_key_windows(dst_ref, dst0, x):
    nmb = len(NA_KSTART)
    for r in range(x.shape[0] // GRID_W):
        jb, r4 = divmod(r, NA_WROWS)
        for m in range(nmb):
            d0 = dst0 + (jb * nmb * NA_WROWS + m * NA_WROWS + r4) * NA_KCOLS
            s0 = r * GRID_W + NA_KSTART[m]
            dst_ref[d0:d0 + NA_KCOLS, :] = x[s0:s0 + NA_KCOLS, :].astype(BF16)


def _odd_inproj_kernel(h_ref, mod_ref, g_ref, win_ref, qg_ref, kg_ref, ones_ref, *refs, kv_only):
    out_refs, z_ref = refs[:-1], refs[-1]
    sub = z_ref.shape[1]
    nsub = h_ref.shape[0] // sub
    ones = ones_ref[...]
    w = MIX_HALF
    slot = lambda t: lax.rem(pl.program_id(1) + t, 2)

    def norm(t):
        z_ref[slot(t)] = _norm_mod(h_ref[t * sub:(t + 1) * sub, :], g_ref[...], mod_ref[3:4, :],
                                   mod_ref[4:5, :]).astype(BF16)

    expand = NA_KCOLS * len(NA_KSTART) // GRID_W
    norm(0)
    for t in range(nsub):
        rows = slice(t * sub, (t + 1) * sub)
        z = z_ref[slot(t)]
        if kv_only:
            k_ref, v_ref = out_refs
            k = _dot(z, win_ref[:, 2 * w:3 * w])
        else:
            xp_ref, q_ref, kw_ref, vw_ref = out_refs
            q = _dot(z, win_ref[:, w:2 * w])
            k = _dot(z, win_ref[:, 2 * w:3 * w])
        if t + 1 < nsub:
            norm(t + 1)
        if not kv_only:
            q_ref[rows, :] = (_head_rms(q, qg_ref[...], ones) * Q_SCALE).astype(BF16)
            xp_ref[rows, :] = _dot(z, win_ref[:, 0:w])
        v = _dot(z, win_ref[:, 3 * w:4 * w])
        k = _head_rms(k, kg_ref[...], ones)
        if kv_only:
            k_ref[rows, :] = k.astype(BF16)
            v_ref[rows, :] = v.astype(BF16)
        else:
            _store_key_windows(kw_ref, t * sub * expand, k)
            _store_key_windows(vw_ref, t * sub * expand, v)


def _odd_inproj(h, mod, mod_row_of_batch, g, win, qg, kg, ones, kv_only):
    b, s, d = h.shape
    tm = min(IN_TILE, s)
    sub = min(IN_SUB, tm)
    expand = NA_KCOLS * len(NA_KSTART) // GRID_W
    tok = lambda w, e=1: pl.BlockSpec((None, e * tm, w), lambda bi, i: (bi, i, 0))
    const = lambda shape: _resident(shape, lambda bi, i: (0,) * len(shape))
    bf = lambda e=1: jax.ShapeDtypeStruct((b, e * s, MIX_HALF), BF16)
    if kv_only:
        out_specs, out_shape = [tok(MIX_HALF)] * 2, [bf(), bf()]
    else:
        assert sub % (NA_WROWS * GRID_W) == 0
        out_specs = [tok(MIX_HALF), tok(MIX_HALF), tok(MIX_HALF, expand), tok(MIX_HALF, expand)]
        out_shape = [jax.ShapeDtypeStruct((b, s, MIX_HALF), F32), bf(), bf(expand), bf(expand)]
    return pl.pallas_call(
        functools.partial(_odd_inproj_kernel, kv_only=kv_only),
        grid=(b, s // tm),
        in_specs=[
            tok(d),
            pl.BlockSpec((None, N_MOD, d), lambda bi, i: (mod_row_of_batch(bi), 0, 0)),
            const((1, d)), const((d, ODD_IN)), const((1, MIX_HALF)), const((1, MIX_HALF)),
            const((MXU_DIM, MXU_DIM)),
        ],
        out_specs=out_specs,
        out_shape=out_shape,
        scratch_shapes=[pltpu.VMEM((2, sub, d), BF16)],
        compiler_params=_params("parallel", "arbitrary"),
        name="odd_inproj_kv" if kv_only else "odd_inproj",
    )(h, mod, g.reshape(1, d), win, qg, kg, ones)


def _na_tables(rows):
    nrb = rows // NA_QROWS
    nc = 2 * NA_COLS - 1
    ridx = np.zeros((3, NA_QROWS, NA_KROWS), np.int32)
    for rv, a in enumerate((0, nrb // 2, nrb - 1)):
        ks = int(np.clip(NA_QROWS * a - NA_ROWS // 2, 0, rows - NA_KROWS))
        r = NA_QROWS * a + np.arange(NA_QROWS)[:, None]
        kr = ks + np.arange(NA_KROWS)[None, :]
        r0 = np.clip(r - NA_ROWS // 2, 0, rows - NA_ROWS)
        ridx[rv] = np.where((kr >= r0) & (kr < r0 + NA_ROWS), kr - r + NA_ROWS - 1, -1)
    csel = np.zeros((3, NA_COLS, NA_KCOLS, nc), np.float32)
    cvalid = np.zeros((3, NA_COLS, NA_KCOLS), bool)
    for cv, m in enumerate((0, 1, 3)):
        qc = NA_COLS * m + np.arange(NA_COLS)[:, None]
        kcol = NA_KSTART[m] + np.arange(NA_KCOLS)[None, :]
        qstart = np.clip(qc - NA_COLS // 2, 0, GRID_W - NA_COLS)
        cvalid[cv] = (kcol >= qstart) & (kcol < qstart + NA_COLS)
        ci = np.clip(kcol - qc, 1 - NA_COLS, NA_COLS - 1) + NA_COLS - 1
        csel[cv] = np.arange(nc) == ci[..., None]
    return ridx, csel, cvalid


def _na_bias_table(rpb, rows):
    ridx, csel, cvalid = _na_tables(rows)
    nr = 2 * NA_ROWS - 1
    nk = NA_KROWS * NA_KCOLS
    cb = jnp.einsum("hab,cqjb->chqaj", rpb, csel, precision=lax.Precision.HIGHEST)
    cb = jnp.where(cvalid[:, None, :, None, :], cb * LOG2E, NEG)
    flat = cb.reshape(cb.shape[:3] + (nr * NA_KCOLS,))
    run = NA_ROWS * NA_KCOLS
    outside = jnp.full(cb.shape[:3] + (nk - run,), NEG, F32)
    variants = []
    for rv in range(ridx.shape[0]):
        strips = []
        for rl in range(NA_QROWS):
            kl0 = int(np.argmax(ridx[rv, rl] >= 0))
            a0 = int(ridx[rv, rl, kl0])
            assert (ridx[rv, rl, kl0:kl0 + NA_ROWS] == a0 + np.arange(NA_ROWS)).all()
            assert (np.delete(ridx[rv, rl], np.s_[kl0:kl0 + NA_ROWS]) < 0).all()
            strips.append(jnp.concatenate([outside[..., :kl0 * NA_KCOLS],
                                           flat[..., a0 * NA_KCOLS:a0 * NA_KCOLS + run],
                                           outside[..., kl0 * NA_KCOLS:]], axis=-1))
        variants.append(jnp.stack(strips, axis=2))
    tbl = jnp.stack(variants, axis=0)
    return tbl.reshape(3, 3, rpb.shape[0], NA_QROWS * NA_COLS, nk)


def _odd_mixer_kernel(xp_ref, xprev_ref, xnext_ref, q_ref, *refs, seq):
    nwb = NA_KROWS // NA_WROWS
    kw_refs, vw_refs = refs[:nwb], refs[nwb:2 * nwb]
    (kc_ref, vc_ref, bias_ref, wpool_ref, cs_ref, wout_ref, mod_ref, h_ref, o_ref, mix_ref,
     xw_ref, ta_ref, tb_ref) = refs[2 * nwb:]
    a = pl.program_id(1)
    last = pl.num_programs(1) - 1
    tq = q_ref.shape[0]
    halo = xprev_ref.shape[0]
    lane = lax.broadcasted_iota(jnp.int32, (1, LANES), 1)
    low = lane < HEAD_DIM

    xw_ref[0:halo, :] = jnp.where(a == 0, 0.0, xprev_ref[...])
    xw_ref[halo:halo + tq, :] = xp_ref[...]
    xw_ref[halo + tq:halo + tq + halo, :] = jnp.where(a == last, 0.0, xnext_ref[...])
    n = tq + 2 * halo
    xw_ref[n:n + halo, :] = jnp.zeros((halo, MIX_HALF), F32)
    ta_ref[n:n + halo, :] = jnp.zeros((halo, C_GROUP_W), F32)
    tb_ref[n:n + halo, :] = jnp.zeros((halo, C_GROUP_W), F32)
    t = a * tq + lax.broadcasted_iota(jnp.int32, (tq, 1), 0)

    def window_sum(cols, w):
        lo = halo - w // 2
        if w < 8:
            acc = None
            for dlt in range(w):
                term = xw_ref[lo + dlt:lo + dlt + tq, cols]
                acc = term if acc is None else acc + term
            return acc
        assert w in (8, 16) and halo == 8
        ta_ref[0:n, :] = xw_ref[0:n, cols] + xw_ref[1:n + 1, cols]
        tb_ref[0:n, :] = ta_ref[0:n, :] + ta_ref[2:n + 2, :]
        ta_ref[0:n, :] = tb_ref[0:n, :] + tb_ref[4:n + 4, :]
        if w == 8:
            return ta_ref[lo:lo + tq, :]
        return ta_ref[0:tq, :] + ta_ref[8:8 + tq, :]

    def pool(gi):
        w = POOL_WINDOWS[gi]
        cols = slice(gi * C_GROUP_W, (gi + 1) * C_GROUP_W)
        acc = window_sum(cols, w)
        cnt =(jnp.minimum(t + (w - w // 2), seq) - jnp.maximum(t - w // 2, 0)).astype(F32)
        pooled = acc / cnt - xp_ref[:, cols]
        y = _dot(pooled.astype(BF16), wpool_ref[gi]) * cs_ref[:, cols]
        mix_ref[:, cols] = y.astype(BF16)

    nmb = GRID_W // NA_COLS
    wrows = NA_WROWS * NA_KCOLS
    nloc = NA_KROWS * NA_KCOLS

    def scores(i):
        p, m = divmod(i, nmb)
        cols = slice(p * LANES, (p + 1) * LANES)
        win = slice(m * wrows, (m + 1) * wrows)
        kcat = jnp.concatenate([r[win, cols] for r in kw_refs] + [kc_ref[:, cols]], axis=0)
        vcat = jnp.concatenate([r[win, cols] for r in vw_refs] + [vc_ref[:, cols]], axis=0)
        qm = jnp.concatenate(
            [q_ref[r * GRID_W + m * NA_COLS:r * GRID_W + (m + 1) * NA_COLS, cols]
             for r in range(NA_QROWS)], axis=0)
        zero = jnp.zeros((), BF16)
        q2 = jnp.concatenate([jnp.where(low, qm, zero), jnp.where(low, zero, qm)], axis=0)
        s = _dot_nt(q2, kcat)
        bias = jnp.concatenate([bias_ref[NA_COLVAR[m], 2 * p], bias_ref[NA_COLVAR[m], 2 * p + 1]], axis=0)
        return jnp.concatenate([s[:, :nloc] + bias, s[:, nloc:]], axis=1), vcat

    def values(i, sv):
        p, m = divmod(i, nmb)
        nq = NA_QROWS * NA_COLS
        o = _softmax_pv(sv[0], sv[1])
        o = jnp.where(low, o[:nq], o[nq:]).astype(BF16)
        for r in range(NA_QROWS):
            mix_ref[r * GRID_W + m * NA_COLS:r * GRID_W + (m + 1) * NA_COLS,
                    MIX_HALF + p * LANES:MIX_HALF + (p + 1) * LANES] = o[r * NA_COLS:(r + 1) * NA_COLS]

    for gi in range(len(POOL_WINDOWS)):
        pool(gi)
    _pipelined((MIX_HALF // LANES) * nmb, QK_AHEAD, scores, values)
    y = _dot(mix_ref[...], wout_ref[...])
    o_ref[...] = h_ref[...] + mod_ref[5:6, :] * y


def _odd_mixer(xp, q, kw, vw, kc, vc, bias_tbl, wpool, cscale, wout, mod, h):
    b, s, d = h.shape
    lc = kc.shape[1]
    tq = NA_QROWS * GRID_W
    nrb = s // tq
    halo = 8
    hb = tq // halo
    nwb = NA_KROWS // NA_WROWS
    wblock = NA_WROWS * NA_KCOLS * len(NA_KSTART)
    tok = lambda w: pl.BlockSpec((None, tq, w), lambda bi, i: (bi, i, 0))
    per_batch = lambda rows, w: _resident((None, rows, w), lambda bi, i: (bi, 0, 0))
    const = lambda shape: _resident(shape, lambda bi, i: (0,) * len(shape))

    def row_variant(i):
        return jnp.where(i == 0, 0, jnp.where(i == nrb - 1, 2, 1))

    def window(j):
        first = lambda i: jnp.clip((NA_QROWS * i - NA_ROWS // 2) // NA_WROWS, 0,
                                   (s // GRID_W - NA_KROWS) // NA_WROWS)
        return pl.BlockSpec((None, wblock, MIX_HALF), lambda bi, i: (bi, first(i) + j, 0))

    return pl.pallas_call(
        functools.partial(_odd_mixer_kernel, seq=s),
        grid=(b, nrb),
        in_specs=[
            tok(MIX_HALF),
            pl.BlockSpec((None, halo, MIX_HALF), lambda bi, i: (bi, jnp.maximum(i * hb - 1, 0), 0)),
            pl.BlockSpec((None, halo, MIX_HALF), lambda bi, i: (bi, jnp.minimum((i + 1) * hb, s // halo - 1), 0)),
            tok(MIX_HALF),
            *[window(j) for j in range(nwb)], *[window(j) for j in range(nwb)],
            per_batch(lc, MIX_HALF), per_batch(lc, MIX_HALF),
            pl.BlockSpec((None,) + bias_tbl.shape[1:], lambda bi, i: (row_variant(i), 0, 0, 0, 0)),
            const(wpool.shape), const(cscale.shape), const((d, d)),
            pl.BlockSpec((None, N_MOD, d), lambda bi, i: (bi, 0, 0)),
            tok(d),
        ],
        out_specs=tok(d),
        out_shape=jax.ShapeDtypeStruct((b, s, d), F32),
        scratch_shapes=[pltpu.VMEM((tq, d), BF16), pltpu.VMEM((tq + 3 * halo, MIX_HALF), F32),
                        pltpu.VMEM((tq + 3 * halo, C_GROUP_W), F32), pltpu.VMEM((tq + 3 * halo, C_GROUP_W), F32)],
        compiler_params=_params("parallel", "arbitrary"),
        name="odd_mixer",
    )(xp, xp, xp, q, *([kw] * nwb), *([vw] * nwb), kc, vc, bias_tbl, wpool, cscale, wout, mod, h)


def _rope_tables(s):
    t = jnp.arange(s)
    row = (t // GRID_W).astype(F32)
    col = (t % GRID_W).astype(F32)
    m = HEAD_DIM // 4
    inv = 1.0 / (ROPE_THETA ** (jnp.arange(m, dtype=F32) / m))
    ang_r = row[:, None] * inv[None, :]
    ang_c = col[:, None] * inv[None, :]
    cos_h = jnp.concatenate([jnp.cos(ang_r)] * 2 + [jnp.cos(ang_c)] * 2, axis=-1)
    sin_h = jnp.concatenate([-jnp.sin(ang_r), jnp.sin(ang_r), -jnp.sin(ang_c), jnp.sin(ang_c)], axis=-1)
    reps = LANES // HEAD_DIM
    return jnp.tile(cos_h, (1, reps)), jnp.tile(sin_h, (1, reps))


def _block_diag_ones():
    idx = np.arange(MXU_DIM) // HEAD_DIM
    return jnp.asarray(idx[:, None] == idx[None, :], dtype=BF16)


def kernel(x, c, ctx, c_ctx, ada_w, ada_b, norm_g, ffn_w_gu, ffn_w_down, ev_w_in, ev_w_out,
           a_q_gain, a_k_gain, a_sink, b_v_gain, b_ws, b_bias, od_w_in, od_w_out, c_w_pool,
           c_scale, d_q_gain, d_k_gain, d_rpb):
    b, s, d = x.shape
    lc = ctx.shape[1]
    depth = ada_w.shape[0]
    assert d == D_MODEL and depth == 2 and s % TOK_TILE == 0 and s // GRID_W >= NA_KROWS
    assert lc % B_CHUNK == 0 and b <= 8

    mod_rows = 16
    cvec = jnp.zeros((mod_rows, d), F32).at[:b].set(c).at[b].set(c_ctx)
    mod = _modulation(cvec, ada_w, ada_b)
    lat_row = lambda bi: bi
    ctx_row = lambda bi: b

    ones = _block_diag_ones()
    cos, sin = _rope_tables(s)
    flat = lambda t: t.reshape(1, b * lc, t.shape[-1])
    unflat = lambda t: t.reshape(b, lc, t.shape[-1])
    cos_id = jnp.ones((b * lc, LANES), F32)
    sin_id = jnp.zeros((b * lc, LANES), F32)
    tile_heads = lambda gain, n: jnp.tile(gain, n).reshape(1, n * HEAD_DIM)

    wgu = ffn_w_gu[0, 0].astype(BF16)
    wd = ffn_w_down[0, 0].astype(BF16)
    cast = lambda nl, nh: (ffn_w_gu, ffn_w_down, nl, nh)
    h, hc = x, flat(ctx)
    li, e = 0, 0
    h, wgu_next, wd_next = _half_ffn(h, mod[li], lat_row, norm_g[li, 0], wgu, wd, 0, cast(li, 1))
    hc = _half_ffn(hc, mod[li], ctx_row, norm_g[li, 0], wgu, wd, 0)
    hsl = lambda hd: slice(hd * HEAD_DIM, (hd + 1) * HEAD_DIM)
    win = jnp.concatenate([ev_w_in[e][:, hsl(hd)] for hd in A_HEAD_PERM] + [ev_w_in[e][:, MIX_HALF:]],
                          axis=1).astype(BF16)
    wout = jnp.concatenate([ev_w_out[e][hsl(hd)] for hd in A_HEAD_PERM] + [ev_w_out[e][MIX_HALF:]],
                           axis=0).astype(BF16)
    qg = tile_heads(a_q_gain[e], N_HEADS)
    kg = tile_heads(a_k_gain[e], A_KV_HEADS)
    vg = b_v_gain[e].reshape(1, MIX_HALF)
    q, k, v, u, vv = _even_inproj(h, mod[li], lat_row, norm_g[li, 1], win, qg, kg, vg, cos, sin, ones)
    qc, kc, vc, uc, vvc = map(unflat, _even_inproj(hc, mod[li], ctx_row, norm_g[li, 1], win, qg, kg, vg,
                                                   cos_id, sin_id, ones))
    sink_cols = jnp.repeat(a_sink[e] * LOG2E, A_BLOCK).reshape(N_HEADS * A_BLOCK, 1)
    ws = b_ws[e].astype(BF16)
    gb = jnp.repeat(b_bias[e].T, HEAD_DIM, axis=1)
    h = _even_mixer(q, k, v, kc, vc, sink_cols, u, vv, ws, gb, wout, mod[li], lat_row, h, True)
    hc = flat(_even_mixer(qc, None, None, kc, vc, sink_cols, uc, vvc, ws, gb, wout, mod[li], ctx_row,
                          unflat(hc), False))
    wgu, wd = wgu_next, wd_next
    h, wgu_next, wd_next = _half_ffn(h, mod[li], lat_row, norm_g[li, 2], wgu, wd, 1, cast(li + 1, 0))
    hc = _half_ffn(hc, mod[li], ctx_row, norm_g[li, 2], wgu, wd, 1)

    li, o = 1, 0
    wgu, wd = wgu_next, wd_next
    h, wgu_next, wd_next = _half_ffn(h, mod[li], lat_row, norm_g[li, 0], wgu, wd, 0, cast(li, 1))
    hc = _half_ffn(hc, mod[li], ctx_row, norm_g[li, 0], wgu, wd, 0)
    win = od_w_in[o].astype(BF16)
    wout = od_w_out[o].astype(BF16)
    qg = tile_heads(d_q_gain[o], N_HEADS)
    kg = tile_heads(d_k_gain[o], N_HEADS)
    xp, q, kw, vw = _odd_inproj(h, mod[li], lat_row, norm_g[li, 1], win, qg, kg, ones, False)
    kc, vc = map(unflat, _odd_inproj(hc, mod[li], ctx_row, norm_g[li, 1], win, qg, kg, ones, True))
    bias_tbl = _na_bias_table(d_rpb[o], s // GRID_W)
    h = _odd_mixer(xp, q, kw, vw, kc, vc, bias_tbl, c_w_pool[o].astype(BF16), c_scale[o].reshape(1, MIX_HALF),
                   wout, mod[li], h)
    h = _half_ffn(h, mod[li], lat_row, norm_g[li, 2], wgu_next, wd_next, 1)
    return h
```

```python
import functools

import numpy as np
import jax
import jax.numpy as jnp
from jax import lax
from jax.experimental import pallas as pl
from jax.experimental.pallas import tpu as pltpu

F32 = jnp.float32
BF16 = jnp.bfloat16

D_MODEL = 1024
GRID_W = 64
HEAD_DIM = 64
MIX_HALF = D_MODEL // 2
N_MOD = 9
D_FF = ((8 * D_MODEL // 3 + 127) // 128) * 128
EPS = 1e-6
ROPE_THETA = 10000.0
NEG = -1e30
N_HEADS = MIX_HALF // HEAD_DIM
A_KV_HEADS = N_HEADS // 4
A_KV = A_KV_HEADS * HEAD_DIM
WINDOW = 128
A_BLOCK = 128
B_CHUNK = 128
EVEN_IN = MIX_HALF + 2 * A_KV + 2 * MIX_HALF
POOL_WINDOWS = (2, 4, 8, 16)
C_GROUP_W = MIX_HALF // len(POOL_WINDOWS)
NA_ROWS = 8
NA_COLS = 16
ODD_IN = 4 * MIX_HALF

LANES = 128
MXU_DIM = 256
VMEM_LIMIT = 56 * 1024 * 1024

LOG2E = 1.4426950408889634
Q_SCALE = HEAD_DIM ** -0.5 * LOG2E
QK_AHEAD = 2
TOK_TILE = 512
FFN_TILE = 1024
FFN_SUB = 512
EVEN_TQ = 512
NA_QROWS = 8
NA_KROWS = 16
NA_KCOLS = 2 * NA_COLS
NA_WROWS = 4
IN_TILE = 1024
IN_SUB = 512
NA_KSTART = tuple(int(v) for v in np.clip(np.arange(GRID_W // NA_COLS) * NA_COLS - NA_COLS // 2,
                                         0, GRID_W - NA_KCOLS))
NA_COLVAR = (0, 1, 1, 2)
FFN_CHUNKS = (512, 512, 512, 512, 512, 256)
assert sum(FFN_CHUNKS) == D_FF

A_HEAD_PERM = (0, 4, 1, 5, 2, 6, 3, 7)


def _params(*sem):
    return pltpu.CompilerParams(dimension_semantics=sem, vmem_limit_bytes=VMEM_LIMIT)


def _resident(shape, index_map):
    return pl.BlockSpec(shape, index_map, pipeline_mode=pl.Buffered(1))


def _norm_mod(h, g, shift, scale):
    y = h * lax.rsqrt(jnp.mean(h * h, axis=-1, keepdims=True) + EPS)
    return (y * g) * (1.0 + scale) + shift


def _dot(a, b):
    return jnp.dot(a, b, preferred_element_type=F32)


def _dot_nt(a, b):
    return lax.dot_general(a, b, (((1,), (1,)), ((), ())), preferred_element_type=F32)


def _head_rms(x, gain, ones_bd):
    w = x.shape[1]
    step = min(w, MXU_DIM)
    ones = ones_bd[:step, :step]
    parts = []
    for c in range(0, w, step):
        xs = x[:, c:c + step]
        ss = _dot((xs * xs).astype(BF16), ones)
        parts.append(xs * lax.rsqrt(ss / HEAD_DIM + EPS))
    y = parts[0] if len(parts) == 1 else jnp.concatenate(parts, axis=1)
    return y * gain


def _rope(x, cos, sin_signed):
    lane = lax.broadcasted_iota(jnp.int32, (1, LANES), 1)
    first = (lane & 16) == 0
    parts = []
    for c in range(0, x.shape[1], LANES):
        xb = x[:, c:c + LANES]
        partner = jnp.where(first, pltpu.roll(xb, LANES - 16, 1), pltpu.roll(xb, 16, 1))
        parts.append(xb * cos + partner * sin_signed)
    return parts[0] if len(parts) == 1 else jnp.concatenate(parts, axis=1)


def _softmax_pv(s, vcat, sink=None, sum_on_mxu=False):
    m = jnp.max(s, axis=-1, keepdims=True)
    if sink is not None:
        m = jnp.maximum(m, sink)
    p = jnp.exp2(s - m)
    if sum_on_mxu:
        o = _dot(p.astype(BF16), jnp.concatenate([vcat, jnp.ones(vcat.shape, BF16)], axis=1))
        num, den = o[:, :LANES], o[:, LANES:]
    else:
        den = jnp.sum(p, axis=-1, keepdims=True)
        num = _dot(p.astype(BF16), vcat)
    if sink is not None:
        den = den + jnp.exp2(sink - m)
    return num / den


def _pipelined(n, ahead, stage_a, stage_b):
    pending = [stage_a(i) for i in range(min(ahead, n))]
    for i in range(n):
        if i + ahead < n:
            pending.append(stage_a(i + ahead))
        stage_b(i, pending.pop(0))


def _mod_kernel(c_ref, w_ref, b_ref, o_ref):
    sc = jax.nn.silu(c_ref[...])
    o_ref[...] = _dot(sc, w_ref[...]) + b_ref[...]


def _modulation(cvec, ada_w, ada_b):
    depth, d, n = ada_w.shape
    r = cvec.shape[0]
    tn = 3072
    out = pl.pallas_call(
        _mod_kernel,
        grid=(depth, n // tn),
        in_specs=[
            pl.BlockSpec((r, d), lambda l, j: (0, 0)),
            pl.BlockSpec((None, d, tn), lambda l, j: (l, 0, j)),
            pl.BlockSpec((None, 1, tn), lambda l, j: (l, 0, j)),
        ],
        out_specs=pl.BlockSpec((None, r, tn), lambda l, j: (l, 0, j)),
        out_shape=jax.ShapeDtypeStruct((depth, r, n), F32),
        compiler_params=_params("parallel", "parallel"),
        name="adaln_mod",
    )(cvec, ada_w, ada_b.reshape(depth, 1, n))
    return out.reshape(depth, r, N_MOD, d)


def _ffn_kernel(h_ref, mod_ref, g_ref, wgu_ref, wd_ref, *refs, mrow, cast_next):
    if cast_next:
        src_gu_ref, src_d_ref, o_ref, dst_gu_ref, dst_d_ref, z_ref, acc_ref = refs
        dst_gu_ref[...] = src_gu_ref[...].astype(BF16)
        dst_d_ref[...] = src_d_ref[...].astype(BF16)
    else:
        o_ref, z_ref, acc_ref = refs
    sub = z_ref.shape[1]
    nsub = h_ref.shape[0] // sub
    slot = lambda t: lax.rem(pl.program_id(1) + t, 2)

    def norm(t):
        h = h_ref[t * sub:(t + 1) * sub, :]
        z_ref[slot(t)] = _norm_mod(h, g_ref[...], mod_ref[mrow:mrow + 1, :],
                                   mod_ref[mrow + 1:mrow + 2, :]).astype(BF16)

    def residual(t):
        rows = slice(t * sub, (t + 1) * sub)
        o_ref[rows, :] = h_ref[rows, :] + (0.5 * mod_ref[mrow + 2:mrow + 3, :]) * acc_ref[slot(t)]

    norm(0)
    for t in range(nsub):
        acc = None
        off = 0
        for ci, ch in enumerate(FFN_CHUNKS):
            z = z_ref[slot(t)]
            gt = _dot(z, wgu_ref[:, off:off + ch])
            up = _dot(z, wgu_ref[:, D_FF + off:D_FF + off + ch])
            a = (jax.nn.silu(gt) * up).astype(BF16)
            part = _dot(a, wd_ref[off:off + ch, :])
            acc = part if acc is None else acc + part
            off += ch
            if ci == 0 and t >= 1:
                residual(t - 1)
            if ci == 1 and t + 1 < nsub:
                norm(t + 1)
        acc_ref[slot(t)] = acc
    residual(nsub - 1)


def _half_ffn(h, mod, mod_row_of_batch, g, wgu, wd, half, cast_next=None):
    b, s, d = h.shape
    tm = min(FFN_TILE, s)
    nt = s // tm
    mrow = 6 * half
    in_specs = [
        pl.BlockSpec((None, tm, d), lambda bi, i: (bi, i, 0)),
        pl.BlockSpec((None, N_MOD, d), lambda bi, i: (mod_row_of_batch(bi), 0, 0)),
        _resident((1, d), lambda bi, i: (0, 0)),
        _resident((d, 2 * D_FF), lambda bi, i: (0, 0)),
        _resident((D_FF, d), lambda bi, i: (0, 0)),
    ]
    args = [h, mod, g.reshape(1, d), wgu, wd]
    out_specs = [pl.BlockSpec((None, tm, d), lambda bi, i: (bi, i, 0))]
    out_shape = [jax.ShapeDtypeStruct((b, s, d), F32)]
    if cast_next is not None:
        src_gu, src_d, nl, nh = cast_next
        steps = b * nt
        rows_gu = d // steps
        rep = 1
        while (D_FF * rep // steps) % 16:
            rep *= 2
        rows_d = D_FF * rep // steps
        assert rows_gu * steps == d and rows_gu % 16 == 0 and rows_d * steps == D_FF * rep and rep <= steps
        in_specs += [
            pl.BlockSpec((None, None, rows_gu, 2 * D_FF), lambda bi, i: (nl, nh, bi * nt + i, 0)),
            pl.BlockSpec((None, None, rows_d, d), lambda bi, i: (nl, nh, (bi * nt + i) // rep, 0)),
        ]
        args += [src_gu, src_d]
        out_specs += [pl.BlockSpec((rows_gu, 2 * D_FF), lambda bi, i: (bi * nt + i, 0)),
                      pl.BlockSpec((rows_d, d), lambda bi, i: ((bi * nt + i) // rep, 0))]
        out_shape += [jax.ShapeDtypeStruct((d, 2 * D_FF), BF16), jax.ShapeDtypeStruct((D_FF, d), BF16)]
    outs = pl.pallas_call(
        functools.partial(_ffn_kernel, mrow=mrow, cast_next=cast_next is not None),
        grid=(b, nt),
        in_specs=in_specs,
        out_specs=out_specs,
        out_shape=out_shape,
        scratch_shapes=[pltpu.VMEM((2, min(FFN_SUB, tm), d), BF16), pltpu.VMEM((2, min(FFN_SUB, tm), d), F32)],
        compiler_params=_params("arbitrary", "arbitrary"),
        name="half_ffn",
    )(*args)
    return outs[0] if cast_next is None else tuple(outs)


def _even_inproj_kernel(h_ref, mod_ref, g_ref, win_ref, qg_ref, kg_ref, vg_ref, cos_ref, sin_ref,
                        ones_ref, q_ref, k_ref, v_ref, u_ref, vv_ref, z_ref):
    sub = z_ref.shape[1]
    nsub = h_ref.shape[0] // sub
    ones = ones_ref[...]
    cq, ck, cv, cu, cb = 0, MIX_HALF, MIX_HALF + A_KV, MIX_HALF + 2 * A_KV, 2 * MIX_HALF + 2 * A_KV
    slot = lambda t: lax.rem(pl.program_id(1) + t, 2)

    def norm(t):
        z_ref[slot(t)] = _norm_mod(h_ref[t * sub:(t + 1) * sub, :], g_ref[...], mod_ref[3:4, :],
                                   mod_ref[4:5, :]).astype(BF16)

    norm(0)
    for t in range(nsub):
        rows = slice(t * sub, (t + 1) * sub)
        z = z_ref[slot(t)]
        cos = cos_ref[rows, :]
        sin = sin_ref[rows, :]
        q = _dot(z, win_ref[:, cq:cq + MIX_HALF])
        k = _dot(z, win_ref[:, ck:ck + A_KV])
        if t + 1 < nsub:
            norm(t + 1)
        q = _rope(_head_rms(q, qg_ref[...], ones), cos, sin)
        q_ref[rows, :] = (q * Q_SCALE).astype(BF16)
        bv = jax.nn.gelu(_dot(z, win_ref[:, cb:cb + MIX_HALF]))
        k_ref[rows, :] = _rope(_head_rms(k, kg_ref[...], ones), cos, sin).astype(BF16)
        v_ref[rows, :] = _dot(z, win_ref[:, cv:cv + A_KV]).astype(BF16)
        u = _dot(z, win_ref[:, cu:cu + MIX_HALF])
        vv_ref[rows, :] = _head_rms(bv, vg_ref[...], ones).astype(BF16)
        u_ref[rows, :] = jax.nn.gelu(u)


def _even_inproj(h, mod, mod_row_of_batch, g, win, qg, kg, vg, cos, sin, ones):
    b, s, d = h.shape
    tm = min(IN_TILE, s)
    tok = lambda w: pl.BlockSpec((None, tm, w), lambda bi, i: (bi, i, 0))
    const = lambda shape: _resident(shape, lambda bi, i: (0,) * len(shape))
    return pl.pallas_call(
        _even_inproj_kernel,
        grid=(b, s // tm),
        in_specs=[
            tok(d),
            pl.BlockSpec((None, N_MOD, d), lambda bi, i: (mod_row_of_batch(bi), 0, 0)),
            const((1, d)), const((d, EVEN_IN)), const((1, MIX_HALF)), const((1, A_KV)),
            const((1, MIX_HALF)),
            pl.BlockSpec((tm, LANES), lambda bi, i: (i, 0)),
            pl.BlockSpec((tm, LANES), lambda bi, i: (i, 0)),
            const((MXU_DIM, MXU_DIM)),
        ],
        out_specs=[tok(MIX_HALF), tok(A_KV), tok(A_KV), tok(MIX_HALF), tok(MIX_HALF)],
        out_shape=[
            jax.ShapeDtypeStruct((b, s, MIX_HALF), BF16),
            jax.ShapeDtypeStruct((b, s, A_KV), BF16),
            jax.ShapeDtypeStruct((b, s, A_KV), BF16),
            jax.ShapeDtypeStruct((b, s, MIX_HALF), F32),
            jax.ShapeDtypeStruct((b, s, MIX_HALF), BF16),
        ],
        scratch_shapes=[pltpu.VMEM((2, min(IN_SUB, tm), d), BF16)],
        compiler_params=_params("parallel", "arbitrary"),
        name="even_inproj",
    )(h, mod, g.reshape(1, d), win, qg, kg, vg, cos, sin, ones)


def _even_mixer_kernel(*refs, local, seq):
    if local:
        (q_ref, k_ref, v_ref, kc_ref, vc_ref, sink_ref, u_ref, vv_ref, ws_ref, gb_ref, wout_ref,
         mod_ref, h_ref, o_ref, mix_ref) = refs
    else:
        (q_ref, kc_ref, vc_ref, sink_ref, u_ref, vv_ref, ws_ref, gb_ref, wout_ref,
         mod_ref, h_ref, o_ref, mix_ref) = refs
    tq = q_ref.shape[0]
    nblk = tq // A_BLOCK
    nloc = 3 * A_BLOCK
    lane = lax.broadcasted_iota(jnp.int32, (1, LANES), 1)
    low = lane < HEAD_DIM
    kc = kc_ref[...]
    vc = vc_ref[...]
    groups = N_HEADS // A_KV_HEADS

    def gmlp(blk):
        r0 = blk * B_CHUNK
        vvb = vv_ref[r0:r0 + B_CHUNK, :]
        for p in range(MIX_HALF // LANES):
            cols = slice(p * LANES, (p + 1) * LANES)
            vsl = vvb[:, cols]
            mixed = jnp.where(low, _dot(ws_ref[2 * p], vsl), _dot(ws_ref[2 * p + 1], vsl)) + gb_ref[:, cols]
            mix_ref[r0:r0 + B_CHUNK, MIX_HALF + p * LANES:MIX_HALF + (p + 1) * LANES] = (
                u_ref[r0:r0 + B_CHUNK, cols] * mixed).astype(BF16)

    def scores(blk):
        r0 = blk * A_BLOCK
        qb = q_ref[r0:r0 + A_BLOCK, :]
        q8 = jnp.concatenate(
            [jnp.where(low if j == 0 else jnp.logical_not(low), qb[:, p * LANES:(p + 1) * LANES],
                       jnp.zeros((), BF16))
             for j in range(A_KV_HEADS) for p in range(groups)], axis=0)
        if not local:
            return _dot_nt(q8, kc), vc
        n = pl.program_id(1) * nblk + blk
        start = pl.multiple_of(jnp.clip((n - 1) * A_BLOCK, 0, seq - nloc), A_BLOCK)
        kcat = jnp.concatenate([k_ref[pl.ds(start, nloc), :], kc], axis=0)
        vcat = jnp.concatenate([v_ref[pl.ds(start, nloc), :], vc], axis=0)
        qpos = n * A_BLOCK + lax.broadcasted_iota(jnp.int32, (A_BLOCK, 1), 0)
        kpos = start + lax.broadcasted_iota(jnp.int32, (1, nloc), 1)
        mask = jnp.where(jnp.abs(qpos - kpos) <= WINDOW, 0.0, NEG)
        s = _dot_nt(q8, kcat)
        s_loc = (s[:, :nloc].reshape(N_HEADS, A_BLOCK, nloc) + mask[None]).reshape(N_HEADS * A_BLOCK, nloc)
        return jnp.concatenate([s_loc, s[:, nloc:]], axis=1), vcat

    def values(blk, sv):
        r0 = blk * A_BLOCK
        o = _softmax_pv(sv[0], sv[1], sink_ref[...], sum_on_mxu=not local)
        for p in range(groups):
            lo_rows = o[p * A_BLOCK:(p + 1) * A_BLOCK]
            hi_rows = o[(groups + p) * A_BLOCK:(groups + p + 1) * A_BLOCK]
            mix_ref[r0:r0 + A_BLOCK, p * LANES:(p + 1) * LANES] = jnp.where(low, lo_rows, hi_rows).astype(BF16)
        tail = min(QK_AHEAD, nblk)
        assert nblk % tail == 0
        if blk == nblk - tail:
            for c in range(nblk):
                gmlp(c)

    _pipelined(nblk, QK_AHEAD, scores, values)
    y = _dot(mix_ref[...], wout_ref[...])
    o_ref[...] = h_ref[...] + mod_ref[5:6, :] * y


def _even_mixer(q, k, v, kc, vc, sink_cols, u, vv, ws, gb, wout, mod, mod_row_of_batch, h, local):
    b, s, d = h.shape
    lc = kc.shape[1]
    tq = min(EVEN_TQ, s)
    tok = lambda w: pl.BlockSpec((None, tq, w), lambda bi, i: (bi, i, 0))
    per_batch = lambda rows, w: pl.BlockSpec((None, rows, w), lambda bi, i: (bi, 0, 0))
    const = lambda shape: _resident(shape, lambda bi, i: (0,) * len(shape))
    in_specs = [tok(MIX_HALF)]
    args = [q]
    if local:
        in_specs += [per_batch(s, A_KV), per_batch(s, A_KV)]
        args += [k, v]
    in_specs += [
        per_batch(lc, A_KV), per_batch(lc, A_KV),
        const(sink_cols.shape),
        tok(MIX_HALF), tok(MIX_HALF),
        const(ws.shape), const(gb.shape), const((d, d)),
        pl.BlockSpec((None, N_MOD, d), lambda bi, i: (mod_row_of_batch(bi), 0, 0)),
        tok(d),
    ]
    args += [kc, vc, sink_cols, u, vv, ws, gb, wout, mod, h]
    return pl.pallas_call(
        functools.partial(_even_mixer_kernel, local=local, seq=s),
        grid=(b, s // tq),
        in_specs=in_specs,
        out_specs=tok(d),
        out_shape=jax.ShapeDtypeStruct((b, s, d), F32),
        scratch_shapes=[pltpu.VMEM((tq, d), BF16)],
        compiler_params=_params("parallel", "parallel"),
        name="even_mixer" if local else "even_mixer_ctx",
    )(*args)


def _store_key_windows(dst_ref, dst0, x):
    nmb = len(NA_KSTART)
    for r in range(x.shape[0] // GRID_W):
        jb, r4 = divmod(r, NA_WROWS)
        for m in range(nmb):
            d0 = dst0 + (jb * nmb * NA_WROWS + m * NA_WROWS + r4) * NA_KCOLS
            s0 = r * GRID_W + NA_KSTART[m]
            dst_ref[d0:d0 + NA_KCOLS, :] = x[s0:s0 + NA_KCOLS, :].astype(BF16)


def _odd_inproj_kernel(h_ref, mod_ref, g_ref, win_ref, qg_ref, kg_ref, ones_ref, *refs, kv_only):
    out_refs, z_ref = refs[:-1], refs[-1]
    sub = z_ref.shape[1]
    nsub = h_ref.shape[0] // sub
    ones = ones_ref[...]
    w = MIX_HALF
    slot = lambda t: lax.rem(pl.program_id(1) + t, 2)

    def norm(t):
        z_ref[slot(t)] = _norm_mod(h_ref[t * sub:(t + 1) * sub, :], g_ref[...], mod_ref[3:4, :],
                                   mod_ref[4:5, :]).astype(BF16)

    expand = NA_KCOLS * len(NA_KSTART) // GRID_W
    norm(0)
    for t in range(nsub):
        rows = slice(t * sub, (t + 1) * sub)
        z = z_ref[slot(t)]
        if kv_only:
            k_ref, v_ref = out_refs
            k = _dot(z, win_ref[:, 2 * w:3 * w])
        else:
            xp_ref, q_ref, kw_ref, vw_ref = out_refs
            q = _dot(z, win_ref[:, w:2 * w])
            k = _dot(z, win_ref[:, 2 * w:3 * w])
        if t + 1 < nsub:
            norm(t + 1)
        if not kv_only:
            q_ref[rows, :] = (_head_rms(q, qg_ref[...], ones) * Q_SCALE).astype(BF16)
            xp_ref[rows, :] = _dot(z, win_ref[:, 0:w])
        v = _dot(z, win_ref[:, 3 * w:4 * w])
        k = _head_rms(k, kg_ref[...], ones)
        if kv_only:
            k_ref[rows, :] = k.astype(BF16)
            v_ref[rows, :] = v.astype(BF16)
        else:
            _store_key_windows(kw_ref, t * sub * expand, k)
            _store_key_windows(vw_ref, t * sub * expand, v)


def _odd_inproj(h, mod, mod_row_of_batch, g, win, qg, kg, ones, kv_only):
    b, s, d = h.shape
    tm = min(IN_TILE, s)
    sub = min(IN_SUB, tm)
    expand = NA_KCOLS * len(NA_KSTART) // GRID_W
    tok = lambda w, e=1: pl.BlockSpec((None, e * tm, w), lambda bi, i: (bi, i, 0))
    const = lambda shape: _resident(shape, lambda bi, i: (0,) * len(shape))
    bf = lambda e=1: jax.ShapeDtypeStruct((b, e * s, MIX_HALF), BF16)
    if kv_only:
        out_specs, out_shape = [tok(MIX_HALF)] * 2, [bf(), bf()]
    else:
        assert sub % (NA_WROWS * GRID_W) == 0
        out_specs = [tok(MIX_HALF), tok(MIX_HALF), tok(MIX_HALF, expand), tok(MIX_HALF, expand)]
        out_shape = [jax.ShapeDtypeStruct((b, s, MIX_HALF), F32), bf(), bf(expand), bf(expand)]
    return pl.pallas_call(
        functools.partial(_odd_inproj_kernel, kv_only=kv_only),
        grid=(b, s // tm),
        in_specs=[
            tok(d),
            pl.BlockSpec((None, N_MOD, d), lambda bi, i: (mod_row_of_batch(bi), 0, 0)),
            const((1, d)), const((d, ODD_IN)), const((1, MIX_HALF)), const((1, MIX_HALF)),
            const((MXU_DIM, MXU_DIM)),
        ],
        out_specs=out_specs,
        out_shape=out_shape,
        scratch_shapes=[pltpu.VMEM((2, sub, d), BF16)],
        compiler_params=_params("parallel", "arbitrary"),
        name="odd_inproj_kv" if kv_only else "odd_inproj",
    )(h, mod, g.reshape(1, d), win, qg, kg, ones)


def _na_tables(rows):
    nrb = rows // NA_QROWS
    nc = 2 * NA_COLS - 1
    ridx = np.zeros((3, NA_QROWS, NA_KROWS), np.int32)
    for rv, a in enumerate((0, nrb // 2, nrb - 1)):
        ks = int(np.clip(NA_QROWS * a - NA_ROWS // 2, 0, rows - NA_KROWS))
        r = NA_QROWS * a + np.arange(NA_QROWS)[:, None]
        kr = ks + np.arange(NA_KROWS)[None, :]
        r0 = np.clip(r - NA_ROWS // 2, 0, rows - NA_ROWS)
        ridx[rv] = np.where((kr >= r0) & (kr < r0 + NA_ROWS), kr - r + NA_ROWS - 1, -1)
    csel = np.zeros((3, NA_COLS, NA_KCOLS, nc), np.float32)
    cvalid = np.zeros((3, NA_COLS, NA_KCOLS), bool)
    for cv, m in enumerate((0, 1, 3)):
        qc = NA_COLS * m + np.arange(NA_COLS)[:, None]
        kcol = NA_KSTART[m] + np.arange(NA_KCOLS)[None, :]
        qstart = np.clip(qc - NA_COLS // 2, 0, GRID_W - NA_COLS)
        cvalid[cv] = (kcol >= qstart) & (kcol < qstart + NA_COLS)
        ci = np.clip(kcol - qc, 1 - NA_COLS, NA_COLS - 1) + NA_COLS - 1
        csel[cv] = np.arange(nc) == ci[..., None]
    return ridx, csel, cvalid


def _na_bias_table(rpb, rows):
    ridx, csel, cvalid = _na_tables(rows)
    nr = 2 * NA_ROWS - 1
    nk = NA_KROWS * NA_KCOLS
    cb = jnp.einsum("hab,cqjb->chqaj", rpb, csel, precision=lax.Precision.HIGHEST)
    cb = jnp.where(cvalid[:, None, :, None, :], cb * LOG2E, NEG)
    flat = cb.reshape(cb.shape[:3] + (nr * NA_KCOLS,))
    run = NA_ROWS * NA_KCOLS
    outside = jnp.full(cb.shape[:3] + (nk - run,), NEG, F32)
    variants = []
    for rv in range(ridx.shape[0]):
        strips = []
        for rl in range(NA_QROWS):
            kl0 = int(np.argmax(ridx[rv, rl] >= 0))
            a0 = int(ridx[rv, rl, kl0])
            assert (ridx[rv, rl, kl0:kl0 + NA_ROWS] == a0 + np.arange(NA_ROWS)).all()
            assert (np.delete(ridx[rv, rl], np.s_[kl0:kl0 + NA_ROWS]) < 0).all()
            strips.append(jnp.concatenate([outside[..., :kl0 * NA_KCOLS],
                                           flat[..., a0 * NA_KCOLS:a0 * NA_KCOLS + run],
                                           outside[..., kl0 * NA_KCOLS:]], axis=-1))
        variants.append(jnp.stack(strips, axis=2))
    tbl = jnp.stack(variants, axis=0)
    return tbl.reshape(3, 3, rpb.shape[0], NA_QROWS * NA_COLS, nk)


def _odd_mixer_kernel(xp_ref, xprev_ref, xnext_ref, q_ref, *refs, seq):
    nwb = NA_KROWS // NA_WROWS
    kw_refs, vw_refs = refs[:nwb], refs[nwb:2 * nwb]
    (kc_ref, vc_ref, bias_ref, wpool_ref, cs_ref, wout_ref, mod_ref, h_ref, o_ref, mix_ref,
     xw_ref, ta_ref, tb_ref) = refs[2 * nwb:]
    a = pl.program_id(1)
    last = pl.num_programs(1) - 1
    tq = q_ref.shape[0]
    halo = xprev_ref.shape[0]
    lane = lax.broadcasted_iota(jnp.int32, (1, LANES), 1)
    low = lane < HEAD_DIM

    xw_ref[0:halo, :] = jnp.where(a == 0, 0.0, xprev_ref[...])
    xw_ref[halo:halo + tq, :] = xp_ref[...]
    xw_ref[halo + tq:halo + tq + halo, :] = jnp.where(a == last, 0.0, xnext_ref[...])
    n = tq + 2 * halo
    xw_ref[n:n + halo, :] = jnp.zeros((halo, MIX_HALF), F32)
    ta_ref[n:n + halo, :] = jnp.zeros((halo, C_GROUP_W), F32)
    tb_ref[n:n + halo, :] = jnp.zeros((halo, C_GROUP_W), F32)
    t = a * tq + lax.broadcasted_iota(jnp.int32, (tq, 1), 0)

    def window_sum(cols, w):
        lo = halo - w // 2
        if w < 8:
            acc = None
            for dlt in range(w):
                term = xw_ref[lo + dlt:lo + dlt + tq, cols]
                acc = term if acc is None else acc + term
            return acc
        assert w in (8, 16) and halo == 8
        ta_ref[0:n, :] = xw_ref[0:n, cols] + xw_ref[1:n + 1, cols]
        tb_ref[0:n, :] = ta_ref[0:n, :] + ta_ref[2:n + 2, :]
        ta_ref[0:n, :] = tb_ref[0:n, :] + tb_ref[4:n + 4, :]
        if w == 8:
            return ta_ref[lo:lo + tq, :]
        return ta_ref[0:tq, :] + ta_ref[8:8 + tq, :]

    def pool(gi):
        w = POOL_WINDOWS[gi]
        cols = slice(gi * C_GROUP_W, (gi + 1) * C_GROUP_W)
        acc = window_sum(cols, w)
        cnt =(jnp.minimum(t + (w - w // 2), seq) - jnp.maximum(t - w // 2, 0)).astype(F32)
        pooled = acc / cnt - xp_ref[:, cols]
        y = _dot(pooled.astype(BF16), wpool_ref[gi]) * cs_ref[:, cols]
        mix_ref[:, cols] = y.astype(BF16)

    nmb = GRID_W // NA_COLS
    wrows = NA_WROWS * NA_KCOLS
    nloc = NA_KROWS * NA_KCOLS

    def scores(i):
        p, m = divmod(i, nmb)
        cols = slice(p * LANES, (p + 1) * LANES)
        win = slice(m * wrows, (m + 1) * wrows)
        kcat = jnp.concatenate([r[win, cols] for r in kw_refs] + [kc_ref[:, cols]], axis=0)
        vcat = jnp.concatenate([r[win, cols] for r in vw_refs] + [vc_ref[:, cols]], axis=0)
        qm = jnp.concatenate(
            [q_ref[r * GRID_W + m * NA_COLS:r * GRID_W + (m + 1) * NA_COLS, cols]
             for r in range(NA_QROWS)], axis=0)
        zero = jnp.zeros((), BF16)
        q2 = jnp.concatenate([jnp.where(low, qm, zero), jnp.where(low, zero, qm)], axis=0)
        s = _dot_nt(q2, kcat)
        bias = jnp.concatenate([bias_ref[NA_COLVAR[m], 2 * p], bias_ref[NA_COLVAR[m], 2 * p + 1]], axis=0)
        return jnp.concatenate([s[:, :nloc] + bias, s[:, nloc:]], axis=1), vcat

    def values(i, sv):
        p, m = divmod(i, nmb)
        nq = NA_QROWS * NA_COLS
        o = _softmax_pv(sv[0], sv[1])
        o = jnp.where(low, o[:nq], o[nq:]).astype(BF16)
        for r in range(NA_QROWS):
            mix_ref[r * GRID_W + m * NA_COLS:r * GRID_W + (m + 1) * NA_COLS,
                    MIX_HALF + p * LANES:MIX_HALF + (p + 1) * LANES] = o[r * NA_COLS:(r + 1) * NA_COLS]

    for gi in range(len(POOL_WINDOWS)):
        pool(gi)
    _pipelined((MIX_HALF // LANES) * nmb, QK_AHEAD, scores, values)
    y = _dot(mix_ref[...], wout_ref[...])
    o_ref[...] = h_ref[...] + mod_ref[5:6, :] * y


def _odd_mixer(xp, q, kw, vw, kc, vc, bias_tbl, wpool, cscale, wout, mod, h):
    b, s, d = h.shape
    lc = kc.shape[1]
    tq = NA_QROWS * GRID_W
    nrb = s // tq
    halo = 8
    hb = tq // halo
    nwb = NA_KROWS // NA_WROWS
    wblock = NA_WROWS * NA_KCOLS * len(NA_KSTART)
    tok = lambda w: pl.BlockSpec((None, tq, w), lambda bi, i: (bi, i, 0))
    per_batch = lambda rows, w: _resident((None, rows, w), lambda bi, i: (bi, 0, 0))
    const = lambda shape: _resident(shape, lambda bi, i: (0,) * len(shape))

    def row_variant(i):
        return jnp.where(i == 0, 0, jnp.where(i == nrb - 1, 2, 1))

    def window(j):
        first = lambda i: jnp.clip((NA_QROWS * i - NA_ROWS // 2) // NA_WROWS, 0,
                                   (s // GRID_W - NA_KROWS) // NA_WROWS)
        return pl.BlockSpec((None, wblock, MIX_HALF), lambda bi, i: (bi, first(i) + j, 0))

    return pl.pallas_call(
        functools.partial(_odd_mixer_kernel, seq=s),
        grid=(b, nrb),
        in_specs=[
            tok(MIX_HALF),
            pl.BlockSpec((None, halo, MIX_HALF), lambda bi, i: (bi, jnp.maximum(i * hb - 1, 0), 0)),
            pl.BlockSpec((None, halo, MIX_HALF), lambda bi, i: (bi, jnp.minimum((i + 1) * hb, s // halo - 1), 0)),
            tok(MIX_HALF),
            *[window(j) for j in range(nwb)], *[window(j) for j in range(nwb)],
            per_batch(lc, MIX_HALF), per_batch(lc, MIX_HALF),
            pl.BlockSpec((None,) + bias_tbl.shape[1:], lambda bi, i: (row_variant(i), 0, 0, 0, 0)),
            const(wpool.shape), const(cscale.shape), const((d, d)),
            pl.BlockSpec((None, N_MOD, d), lambda bi, i: (bi, 0, 0)),
            tok(d),
        ],
        out_specs=tok(d),
        out_shape=jax.ShapeDtypeStruct((b, s, d), F32),
        scratch_shapes=[pltpu.VMEM((tq, d), BF16), pltpu.VMEM((tq + 3 * halo, MIX_HALF), F32),
                        pltpu.VMEM((tq + 3 * halo, C_GROUP_W), F32), pltpu.VMEM((tq + 3 * halo, C_GROUP_W), F32)],
        compiler_params=_params("parallel", "arbitrary"),
        name="odd_mixer",
    )(xp, xp, xp, q, *([kw] * nwb), *([vw] * nwb), kc, vc, bias_tbl, wpool, cscale, wout, mod, h)


def _rope_tables(s):
    t = jnp.arange(s)
    row = (t // GRID_W).astype(F32)
    col = (t % GRID_W).astype(F32)
    m = HEAD_DIM // 4
    inv = 1.0 / (ROPE_THETA ** (jnp.arange(m, dtype=F32) / m))
    ang_r = row[:, None] * inv[None, :]
    ang_c = col[:, None] * inv[None, :]
    cos_h = jnp.concatenate([jnp.cos(ang_r)] * 2 + [jnp.cos(ang_c)] * 2, axis=-1)
    sin_h = jnp.concatenate([-jnp.sin(ang_r), jnp.sin(ang_r), -jnp.sin(ang_c), jnp.sin(ang_c)], axis=-1)
    reps = LANES // HEAD_DIM
    return jnp.tile(cos_h, (1, reps)), jnp.tile(sin_h, (1, reps))


def _block_diag_ones():
    idx = np.arange(MXU_DIM) // HEAD_DIM
    return jnp.asarray(idx[:, None] == idx[None, :], dtype=BF16)


def kernel(x, c, ctx, c_ctx, ada_w, ada_b, norm_g, ffn_w_gu, ffn_w_down, ev_w_in, ev_w_out,
           a_q_gain, a_k_gain, a_sink, b_v_gain, b_ws, b_bias, od_w_in, od_w_out, c_w_pool,
           c_scale, d_q_gain, d_k_gain, d_rpb):
    b, s, d = x.shape
    lc = ctx.shape[1]
    depth = ada_w.shape[0]
    assert d == D_MODEL and depth == 2 and s % TOK_TILE == 0 and s // GRID_W >= NA_KROWS
    assert lc % B_CHUNK == 0 and b <= 8

    mod_rows = 16
    cvec = jnp.zeros((mod_rows, d), F32).at[:b].set(c).at[b].set(c_ctx)
    mod = _modulation(cvec, ada_w, ada_b)
    lat_row = lambda bi: bi
    ctx_row = lambda bi: b

    ones = _block_diag_ones()
    cos, sin = _rope_tables(s)
    flat = lambda t: t.reshape(1, b * lc, t.shape[-1])
    unflat = lambda t: t.reshape(b, lc, t.shape[-1])
    cos_id = jnp.ones((b * lc, LANES), F32)
    sin_id = jnp.zeros((b * lc, LANES), F32)
    tile_heads = lambda gain, n: jnp.tile(gain, n).reshape(1, n * HEAD_DIM)

    wgu = ffn_w_gu[0, 0].astype(BF16)
    wd = ffn_w_down[0, 0].astype(BF16)
    cast = lambda nl, nh: (ffn_w_gu, ffn_w_down, nl, nh)
    h, hc = x, flat(ctx)
    li, e = 0, 0
    h, wgu_next, wd_next = _half_ffn(h, mod[li], lat_row, norm_g[li, 0], wgu, wd, 0, cast(li, 1))
    hc = _half_ffn(hc, mod[li], ctx_row, norm_g[li, 0], wgu, wd, 0)
    hsl = lambda hd: slice(hd * HEAD_DIM, (hd + 1) * HEAD_DIM)
    win = jnp.concatenate([ev_w_in[e][:, hsl(hd)] for hd in A_HEAD_PERM] + [ev_w_in[e][:, MIX_HALF:]],
                          axis=1).astype(BF16)
    wout = jnp.concatenate([ev_w_out[e][hsl(hd)] for hd in A_HEAD_PERM] + [ev_w_out[e][MIX_HALF:]],
                           axis=0).astype(BF16)
    qg = tile_heads(a_q_gain[e], N_HEADS)
    kg = tile_heads(a_k_gain[e], A_KV_HEADS)
    vg = b_v_gain[e].reshape(1, MIX_HALF)
    q, k, v, u, vv = _even_inproj(h, mod[li], lat_row, norm_g[li, 1], win, qg, kg, vg, cos, sin, ones)
    qc, kc, vc, uc, vvc = map(unflat, _even_inproj(hc, mod[li], ctx_row, norm_g[li, 1], win, qg, kg, vg,
                                                   cos_id, sin_id, ones))
    sink_cols = jnp.repeat(a_sink[e] * LOG2E, A_BLOCK).reshape(N_HEADS * A_BLOCK, 1)
    ws = b_ws[e].astype(BF16)
    gb = jnp.repeat(b_bias[e].T, HEAD_DIM, axis=1)
    h = _even_mixer(q, k, v, kc, vc, sink_cols, u, vv, ws, gb, wout, mod[li], lat_row, h, True)
    hc = flat(_even_mixer(qc, None, None, kc, vc, sink_cols, uc, vvc, ws, gb, wout, mod[li], ctx_row,
                          unflat(hc), False))
    wgu, wd = wgu_next, wd_next
    h, wgu_next, wd_next = _half_ffn(h, mod[li], lat_row, norm_g[li, 2], wgu, wd, 1, cast(li + 1, 0))
    hc = _half_ffn(hc, mod[li], ctx_row, norm_g[li, 2], wgu, wd, 1)

    li, o = 1, 0
    wgu, wd = wgu_next, wd_next
    h, wgu_next, wd_next = _half_ffn(h, mod[li], lat_row, norm_g[li, 0], wgu, wd, 0, cast(li, 1))
    hc = _half_ffn(hc, mod[li], ctx_row, norm_g[li, 0], wgu, wd, 0)
    win = od_w_in[o].astype(BF16)
    wout = od_w_out[o].astype(BF16)
    qg = tile_heads(d_q_gain[o], N_HEADS)
    kg = tile_heads(d_k_gain[o], N_HEADS)
    xp, q, kw, vw = _odd_inproj(h, mod[li], lat_row, norm_g[li, 1], win, qg, kg, ones, False)
    kc, vc = map(unflat, _odd_inproj(hc, mod[li], ctx_row, norm_g[li, 1], win, qg, kg, ones, True))
    bias_tbl = _na_bias_table(d_rpb[o], s // GRID_W)
    h = _odd_mixer(xp, q, kw, vw, kc, vc, bias_tbl, c_w_pool[o].astype(BF16), c_scale[o].reshape(1, MIX_HALF),
                   wout, mod[li], h)
    h = _half_ffn(h, mod[li], lat_row, norm_g[li, 2], wgu_next, wd_next, 1)
    return h
```
